```python
import math
import jax, jax.numpy as jnp
from jax import lax
import numpy as np

D_MODEL = 1024
BATCH = 8
SEQ = 2048
DEPTH = 4
DEC_BATCH = 128
DEC_SEQ = 4
PAST_LEN = 8192
PAGE_SIZE = 128

N_EVEN = (DEPTH + 1) // 2
N_ODD = DEPTH // 2
EPS = 1e-6

GDN_HEADS = 4
GDN_DK = 128
GDN_DV = 128
GDN_CONV = 4
GDN_CHUNK = 64
GDN_QK = GDN_HEADS * GDN_DK
GDN_VW = GDN_HEADS * GDN_DV
GDN_CONV_CH = 2 * GDN_QK + GDN_VW

MOBA_HEADS = 4
MOBA_KV_HEADS = 2
MOBA_DH = 128
MOBA_BLOCK = 256
MOBA_TOPK = 3
MOBA_Q_CHUNK = 32
MOBA_QW = MOBA_HEADS * MOBA_DH
MOBA_KVW = MOBA_KV_HEADS * MOBA_DH

EVEN_SPLITS = (GDN_CONV_CH, GDN_VW, GDN_HEADS, GDN_HEADS, MOBA_QW, MOBA_KVW, MOBA_KVW, MOBA_QW)
EVEN_IN = sum(EVEN_SPLITS)
EVEN_MIX = GDN_VW + MOBA_QW

MLA_HEADS = 8
MLA_Q_RANK = 384
MLA_KV_RANK = 256
MLA_NOPE = 128
MLA_ROPE = 64
MLA_V = 128
MLA_Q_BLOCK = 128
ROPE_THETA = 10000.0
ODD_MIX = MLA_HEADS * MLA_V
ODD_SPLITS = (MLA_Q_RANK, MLA_KV_RANK, MLA_ROPE, ODD_MIX)
ODD_IN = sum(ODD_SPLITS)

kernel_name = 'hybrid_gdn_moba_mla_step'


def _split(x, sizes):
    return jnp.split(x, np.cumsum(sizes)[:-1].tolist(), axis=-1)


def rmsnorm(x, g):
    xf = x.astype(jnp.float32)
    y = xf * lax.rsqrt(jnp.mean(xf * xf, axis=-1, keepdims=True) + EPS)
    return (y * g.astype(jnp.float32)).astype(x.dtype)


def l2norm(x):
    xf = x.astype(jnp.float32)
    return xf * lax.rsqrt(jnp.sum(xf * xf, axis=-1, keepdims=True) + EPS)


def rope(x, pos):
    half = x.shape[-1] // 2
    inv = ROPE_THETA ** (-jnp.arange(half, dtype=jnp.float32) / half)
    ang = pos.astype(jnp.float32)[:, None] * inv[None, :]
    shape = (1, pos.shape[0]) + (1,) * (x.ndim - 3) + (half,)
    cos = jnp.cos(ang).reshape(shape)
    sin = jnp.sin(ang).reshape(shape)
    xf = x.astype(jnp.float32)
    x1, x2 = xf[..., :half], xf[..., half:]
    return jnp.concatenate([x1 * cos - x2 * sin, x1 * sin + x2 * cos], axis=-1).astype(x.dtype)


def _chunk(a, n, c):
    t = a.shape[1]
    a = jnp.pad(a, [(0, 0), (0, n * c - t)] + [(0, 0)] * (a.ndim - 2))
    return jnp.moveaxis(a.reshape((a.shape[0], n, c) + a.shape[2:]), 1, 0)


def _unchunk(a, t):
    a = jnp.moveaxis(a, 0, 1)
    return a.reshape((a.shape[0], -1) + a.shape[3:])[:, :t]


def _cat_pad(parts, multiple):
    length = sum(p.shape[1] for p in parts)
    pad = -length % multiple
    parts = list(parts)
    if pad:
        parts.append(jnp.zeros((parts[0].shape[0], pad) + parts[0].shape[2:], parts[-1].dtype))
    return parts[0] if len(parts) == 1 else jnp.concatenate(parts, axis=1)


def short_conv(u, buf, w):
    t = u.shape[1]
    ext = jnp.concatenate([buf.astype(u.dtype), u], axis=1)
    y = ext[:, GDN_CONV - 1:] * w[GDN_CONV - 1]
    for i in range(GDN_CONV - 1):
        y = y + ext[:, i:i + t] * w[i]
    return jax.nn.silu(y), ext[:, t:]


def gdn_chunked(q, k, v, g, beta, s0):
    t = q.shape[1]
    c = min(GDN_CHUNK, t)
    n = -(-t // c)
    xs = tuple(_chunk(a, n, c) for a in (q, k, v, g, beta))
    tri = jnp.tril(jnp.ones((c, c), dtype=bool))
    strict = jnp.tril(jnp.ones((c, c), dtype=bool), -1)
    eye = jnp.eye(c, dtype=jnp.float32)

    def step(s, inp):
        qc, kc, vc, gc, bc = inp
        gcum = jnp.cumsum(gc, axis=1)
        gh = jnp.moveaxis(gcum, 2, 1)
        decay = jnp.exp(jnp.where(tri, gh[..., :, None] - gh[..., None, :], -jnp.inf))
        kb = kc * bc[..., None]
        lmat = eye + jnp.where(strict, jnp.einsum('bihd,bjhd->bhij', kb, kc) * decay, 0.0)
        rhs = jnp.moveaxis(jnp.concatenate([vc * bc[..., None], kb * jnp.exp(gcum)[..., None]], axis=-1), 2, 1)
        sol = lax.linalg.triangular_solve(lmat, rhs, left_side=True, lower=True, unit_diagonal=True)
        u = sol[..., :GDN_DV] - jnp.einsum('bhik,bhkv->bhiv', sol[..., GDN_DV:], s)
        qh = jnp.moveaxis(qc, 2, 1)
        qk = jnp.einsum('bihd,bjhd->bhij', qc, kc) * decay
        o = (jnp.einsum('bhik,bhkv->bhiv', qh * jnp.exp(gh)[..., None], s)
             + jnp.einsum('bhij,bhjv->bhiv', qk, u))
        g_last = gcum[:, -1]
        kw = kc * jnp.exp(g_last[:, None] - gcum)[..., None]
        s = s * jnp.exp(g_last)[..., None, None] + jnp.einsum('bjhk,bhjv->bhkv', kw, u)
        return s, jnp.moveaxis(o, 1, 2)

    s_fin, out = lax.scan(step, s0, xs)
    return _unchunk(out, t), s_fin


def moba_attend(q, k_parts, v_parts, q_pos):
    b, t, hq, dh = q.shape
    kp = _cat_pad(k_parts, MOBA_BLOCK)
    vp = _cat_pad(v_parts, MOBA_BLOCK)
    nblk = kp.shape[1] // MOBA_BLOCK
    kblk = kp.reshape(b, nblk, MOBA_BLOCK, MOBA_KV_HEADS, dh)
    vblk = vp.reshape(b, nblk, MOBA_BLOCK, MOBA_KV_HEADS, dh)
    head_kv = jnp.arange(hq) // (hq // MOBA_KV_HEADS)
    kmean = jnp.mean(kblk, axis=2, dtype=jnp.float32)[:, :, head_kv]
    topk = min(MOBA_TOPK, nblk)
    c = min(MOBA_Q_CHUNK, t)
    n = -(-t // c)
    qs = _chunk(q, n, c)
    ps = jnp.pad(q_pos, (0, n * c - t), mode='edge').reshape(n, c)
    bi = jnp.arange(b)[:, None, None, None]
    hi = head_kv[None, :, None, None]
    blk_ids = jnp.arange(nblk)
    offs = jnp.arange(MOBA_BLOCK)
    scale = dh ** -0.5

    def one(args):
        qc, pc = args
        qh = jnp.moveaxis(qc, 2, 1).astype(jnp.float32)
        cur = pc // MOBA_BLOCK
        gate = jnp.einsum('bhqd,bnhd->bhqn', qh, kmean)
        gate = jnp.where(blk_ids[None, None, None, :] < cur[None, None, :, None], gate, -jnp.inf)
        _, sel = lax.top_k(gate, topk)
        sel_ok = sel < cur[None, None, :, None]
        ksel = kblk[bi, sel, :, hi].astype(jnp.float32)
        vsel = vblk[bi, sel, :, hi].astype(jnp.float32)
        own = cur[:, None] * MOBA_BLOCK + offs[None, :]
        kown = kp[bi, own[None, None], hi].astype(jnp.float32)
        vown = vp[bi, own[None, None], hi].astype(jnp.float32)
        s_sel = jnp.einsum('bhqd,bhqknd->bhqkn', qh, ksel) * scale
        s_sel = jnp.where(sel_ok[..., None], s_sel, -jnp.inf).reshape(b, hq, c, topk * MOBA_BLOCK)
        s_own = jnp.einsum('bhqd,bhqnd->bhqn', qh, kown) * scale
        s_own = jnp.where(own[None, None] <= pc[None, None, :, None], s_own, -jnp.inf)
        p = jax.nn.softmax(jnp.concatenate([s_sel, s_own], axis=-1), axis=-1)
        p_sel = p[..., :topk * MOBA_BLOCK].reshape(b, hq, c, topk, MOBA_BLOCK)
        return (jnp.einsum('bhqkn,bhqknd->bqhd', p_sel, vsel)
                + jnp.einsum('bhqn,bhqnd->bqhd', p[..., topk * MOBA_BLOCK:], vown))

    out = lax.map(one, (qs, ps))
    return _unchunk(out, t).astype(q.dtype)


def mla_attend(q_lat, q_rope, ckv, krope, q_pos):
    t = q_lat.shape[1]
    c = min(MLA_Q_BLOCK, t)
    n = -(-t // c)
    qls = _chunk(q_lat.astype(jnp.float32), n, c)
    qrs = _chunk(q_rope.astype(jnp.float32), n, c)
    ps = jnp.pad(q_pos, (0, n * c - t), mode='edge').reshape(n, c)
    ckv_f = ckv.astype(jnp.float32)
    kr_f = krope.astype(jnp.float32)
    kpos = jnp.arange(ckv.shape[1])
    scale = (MLA_NOPE + MLA_ROPE) ** -0.5

    def one(args):
        ql, qr, pc = args
        s = (jnp.einsum('bqhr,blr->bhql', ql, ckv_f) + jnp.einsum('bqhp,blp->bhql', qr, kr_f)) * scale
        s = jnp.where(kpos[None, None, None, :] <= pc[None, None, :, None], s, -jnp.inf)
        return jnp.einsum('bhql,blr->bqhr', jax.nn.softmax(s, axis=-1), ckv_f)

    return _unchunk(lax.map(one, (qls, qrs, ps)), t)


def even_layer(h, pos, s0, buf0, k_past, v_past, w_in, conv_w, a_log, dt_bias, gdn_norm, w_out):
    b, t, _ = h.shape
    u = jnp.einsum('btd,de->bte', h, w_in)
    qkv_a, z_a, b_a, a_a, q_b, k_b, v_b, z_b = _split(u, EVEN_SPLITS)
    qkv_c, new_buf = short_conv(qkv_a, buf0, conv_w)
    qa, ka, va = _split(qkv_c, (GDN_QK, GDN_QK, GDN_VW))
    qa = l2norm(qa.reshape(b, t, GDN_HEADS, GDN_DK)) * GDN_DK ** -0.5
    ka = l2norm(ka.reshape(b, t, GDN_HEADS, GDN_DK))
    va = va.reshape(b, t, GDN_HEADS, GDN_DV).astype(jnp.float32)
    beta = jax.nn.sigmoid(b_a.astype(jnp.float32))
    g = -jnp.exp(a_log.astype(jnp.float32)) * jax.nn.softplus(a_a.astype(jnp.float32) + dt_bias.astype(jnp.float32))
    o_a, s_new = gdn_chunked(qa, ka, va, g, beta, s0.astype(jnp.float32))
    o_a = rmsnorm(o_a, gdn_norm) * jax.nn.silu(z_a.reshape(b, t, GDN_HEADS, GDN_DV).astype(jnp.float32))
    o_a = o_a.reshape(b, t, GDN_VW).astype(h.dtype)
    qb = q_b.reshape(b, t, MOBA_HEADS, MOBA_DH)
    kb = k_b.reshape(b, t, MOBA_KV_HEADS, MOBA_DH)
    vb = v_b.reshape(b, t, MOBA_KV_HEADS, MOBA_DH)
    k_parts = [kb] if k_past is None else [k_past, kb]
    v_parts = [vb] if v_past is None else [v_past, vb]
    o_b = moba_attend(qb, k_parts, v_parts, pos).reshape(b, t, MOBA_QW) * jax.nn.silu(z_b)
    out = jnp.einsum('bte,ed->btd', jnp.concatenate([o_a, o_b], axis=-1), w_out)
    return out, s_new.astype(s0.dtype), new_buf, kb, vb


def odd_layer(h, pos, ckv_past, kr_past, w_in, q_norm, kv_norm, w_uq, w_ukv, w_out):
    b, t, _ = h.shape
    u = jnp.einsum('btd,de->bte', h, w_in)
    cq, c_kv, k_r, z = _split(u, ODD_SPLITS)
    q = jnp.einsum('btr,re->bte', rmsnorm(cq, q_norm), w_uq).reshape(b, t, MLA_HEADS, MLA_NOPE + MLA_ROPE)
    q_nope, q_rope = q[..., :MLA_NOPE], rope(q[..., MLA_NOPE:], pos)
    ckv = rmsnorm(c_kv, kv_norm)
    krope = rope(k_r, pos)
    w_ukv_h = w_ukv.reshape(MLA_KV_RANK, MLA_HEADS, MLA_NOPE + MLA_V)
    w_uk, w_uv = w_ukv_h[..., :MLA_NOPE], w_ukv_h[..., MLA_NOPE:]
    q_lat = jnp.einsum('bthn,rhn->bthr', q_nope, w_uk)
    ckv_all = ckv if ckv_past is None else jnp.concatenate([ckv_past, ckv], axis=1)
    kr_all = krope if kr_past is None else jnp.concatenate([kr_past, krope], axis=1)
    o_lat = mla_attend(q_lat, q_rope, ckv_all, kr_all, pos)
    o = jnp.einsum('bthr,rhv->bthv', o_lat, w_uv.astype(jnp.float32)).reshape(b, t, ODD_MIX).astype(h.dtype)
    o = o * jax.nn.silu(z)
    return jnp.einsum('bte,ed->btd', o, w_out), ckv, krope


def setup_inputs(seed: int = 0) -> dict:
    key = jax.random.key(seed)
    ks = iter(jax.random.split(key, 32))
    f32 = jnp.float32

    def nrm(shape, s):
        return jax.random.normal(next(ks), shape, f32) * s

    n_pages = PAST_LEN // PAGE_SIZE
    n_used = DEC_BATCH * n_pages
    n_pool = n_used + n_used // 4
    x_prompt = nrm((BATCH, SEQ, D_MODEL), 1.0)
    x_sample = nrm((DEC_BATCH, DEC_SEQ, D_MODEL), 1.0)
    state_gdn = nrm((N_EVEN, DEC_BATCH, GDN_HEADS, GDN_DK, GDN_DV), 0.1)
    state_conv = nrm((N_EVEN, DEC_BATCH, GDN_CONV - 1, GDN_CONV_CH), 1.0)
    cache_moba_k = nrm((N_EVEN, n_pool, PAGE_SIZE, MOBA_KV_HEADS, MOBA_DH), 1.0)
    cache_moba_v = nrm((N_EVEN, n_pool, PAGE_SIZE, MOBA_KV_HEADS, MOBA_DH), 1.0)
    cache_mla_ckv = nrm((N_ODD, n_pool, PAGE_SIZE, MLA_KV_RANK), 1.0)
    cache_mla_krope = nrm((N_ODD, n_pool, PAGE_SIZE, MLA_ROPE), 1.0)
    perm = jax.random.permutation(next(ks), n_pool)
    page_table = perm[:n_used].reshape(DEC_BATCH, n_pages).astype(jnp.int32)
    norm_pre = 1.0 + nrm((DEPTH, D_MODEL), 0.02)
    norm_post = 1.0 + nrm((DEPTH, D_MODEL), 0.02)
    even_w_in = nrm((N_EVEN, D_MODEL, EVEN_IN), D_MODEL ** -0.5)
    even_conv_w = nrm((N_EVEN, GDN_CONV, GDN_CONV_CH), GDN_CONV ** -0.5)
    even_a_log = jnp.log(jax.random.uniform(next(ks), (N_EVEN, GDN_HEADS), f32, 1.0, 16.0))
    dt = jnp.exp(jax.random.uniform(next(ks), (N_EVEN, GDN_HEADS), f32, math.log(1e-3), math.log(1e-1)))
    even_dt_bias = dt + jnp.log(-jnp.expm1(-dt))
    even_gdn_norm = 1.0 + nrm((N_EVEN, GDN_DV), 0.02)
    even_w_out = nrm((N_EVEN, EVEN_MIX, D_MODEL), EVEN_MIX ** -0.5)
    odd_w_in = nrm((N_ODD, D_MODEL, ODD_IN), D_MODEL ** -0.5)
    odd_q_norm = 1.0 + nrm((N_ODD, MLA_Q_RANK), 0.02)
    odd_kv_norm = 1.0 + nrm((N_ODD, MLA_KV_RANK), 0.02)
    odd_w_uq = nrm((N_ODD, MLA_Q_RANK, MLA_HEADS * (MLA_NOPE + MLA_ROPE)), MLA_Q_RANK ** -0.5)
    odd_w_ukv = nrm((N_ODD, MLA_KV_RANK, MLA_HEADS * (MLA_NOPE + MLA_V)), MLA_KV_RANK ** -0.5)
    odd_w_out = nrm((N_ODD, ODD_MIX, D_MODEL), ODD_MIX ** -0.5)
    return {'x_prompt': x_prompt, 'x_sample': x_sample, 'state_gdn': state_gdn, 'state_conv': state_conv,
            'cache_moba_k': cache_moba_k, 'cache_moba_v': cache_moba_v, 'cache_mla_ckv': cache_mla_ckv,
            'cache_mla_krope': cache_mla_krope, 'page_table': page_table,
            'norm_pre': norm_pre, 'norm_post': norm_post,
            'even_w_in': even_w_in, 'even_conv_w': even_conv_w, 'even_a_log': even_a_log,
            'even_dt_bias': even_dt_bias, 'even_gdn_norm': even_gdn_norm, 'even_w_out': even_w_out,
            'odd_w_in': odd_w_in, 'odd_q_norm': odd_q_norm, 'odd_kv_norm': odd_kv_norm,
            'odd_w_uq': odd_w_uq, 'odd_w_ukv': odd_w_ukv, 'odd_w_out': odd_w_out}


def reference(x_prompt, x_sample, state_gdn, state_conv, cache_moba_k, cache_moba_v, cache_mla_ckv,
              cache_mla_krope, page_table, norm_pre, norm_post, even_w_in, even_conv_w, even_a_log,
              even_dt_bias, even_gdn_norm, even_w_out, odd_w_in, odd_q_norm, odd_kv_norm, odd_w_uq,
              odd_w_ukv, odd_w_out):
    b, t, _ = x_prompt.shape
    db, ds, _ = x_sample.shape
    past_len = page_table.shape[1] * PAGE_SIZE
    pos_p = jnp.arange(t, dtype=jnp.int32)
    pos_s = past_len + jnp.arange(ds, dtype=jnp.int32)

    def gather_pages(pool, i):
        g = pool[i, page_table]
        return g.reshape((db, past_len) + pool.shape[3:])

    yp, ys = x_prompt, x_sample
    p_gdn, p_conv, p_k, p_v, p_ckv, p_kr = [], [], [], [], [], []
    s_gdn, s_conv, s_k, s_v, s_ckv, s_kr = [], [], [], [], [], []
    for l in range(DEPTH):
        i = l // 2
        hp = rmsnorm(yp, norm_pre[l])
        hs = rmsnorm(ys, norm_pre[l])
        if l % 2 == 0:
            w = (even_w_in[i], even_conv_w[i], even_a_log[i], even_dt_bias[i], even_gdn_norm[i], even_w_out[i])
            s0 = jnp.zeros((b, GDN_HEADS, GDN_DK, GDN_DV), state_gdn.dtype)
            buf0 = jnp.zeros((b, GDN_CONV - 1, GDN_CONV_CH), x_prompt.dtype)
            op, st, bf, kk, vv = even_layer(hp, pos_p, s0, buf0, None, None, *w)
            p_gdn.append(st); p_conv.append(bf); p_k.append(kk); p_v.append(vv)
            os_, st, bf, kk, vv = even_layer(hs, pos_s, state_gdn[i], state_conv[i],
                                             gather_pages(cache_moba_k, i), gather_pages(cache_moba_v, i), *w)
            s_gdn.append(st); s_conv.append(bf); s_k.append(kk); s_v.append(vv)
        else:
            w = (odd_w_in[i], odd_q_norm[i], odd_kv_norm[i], odd_w_uq[i], odd_w_ukv[i], odd_w_out[i])
            op, c1, r1 = odd_layer(hp, pos_p, None, None, *w)
            p_ckv.append(c1); p_kr.append(r1)
            os_, c1, r1 = odd_layer(hs, pos_s, gather_pages(cache_mla_ckv, i), gather_pages(cache_mla_krope, i), *w)
            s_ckv.append(c1); s_kr.append(r1)
        yp = yp + rmsnorm(op, norm_post[l])
        ys = ys + rmsnorm(os_, norm_post[l])
    return (yp, ys, jnp.stack(p_gdn), jnp.stack(p_conv), jnp.stack(p_k), jnp.stack(p_v),
            jnp.stack(p_ckv), jnp.stack(p_kr), jnp.stack(s_gdn), jnp.stack(s_conv), jnp.stack(s_k),
            jnp.stack(s_v), jnp.stack(s_ckv), jnp.stack(s_kr))
```

```python
import functools
import math

import jax
import jax.numpy as jnp
from jax import lax
from jax.experimental import pallas as pl
from jax.experimental.pallas import tpu as pltpu

F32 = jnp.float32
BF16 = jnp.bfloat16
HI = lax.Precision.HIGHEST
EPS = 1e-6
NEG_INF = float("-inf")

LANES = 128
SUBLANES = 8
VMEM_LIMIT = 56 * 1024 * 1024

PAGE_SIZE = 128
GDN_HEADS = 4
GDN_DK = 128
GDN_DV = 128
GDN_CONV = 4
GDN_CHUNK = 64
GDN_QK = GDN_HEADS * GDN_DK
GDN_VW = GDN_HEADS * GDN_DV
GDN_CONV_CH = 2 * GDN_QK + GDN_VW
MOBA_HEADS = 4
MOBA_KV_HEADS = 2
MOBA_DH = 128
MOBA_BLOCK = 256
MOBA_TOPK = 3
MOBA_QW = MOBA_HEADS * MOBA_DH
MOBA_KVW = MOBA_KV_HEADS * MOBA_DH
MLA_HEADS = 8
MLA_Q_RANK = 384
MLA_KV_RANK = 256
MLA_NOPE = 128
MLA_ROPE = 64
MLA_V = 128
ROPE_THETA = 10000.0
ODD_MIX = MLA_HEADS * MLA_V
MLA_KCAT = MLA_KV_RANK + LANES


def _params(*sem):
    return pltpu.CompilerParams(dimension_semantics=sem, vmem_limit_bytes=VMEM_LIMIT)


def _dot(a, b, prec=None):
    return jnp.dot(a, b, preferred_element_type=F32, precision=prec)


def _dot_nt(a, b, prec=None):
    return lax.dot_general(a, b, (((1,), (1,)), ((), ())), preferred_element_type=F32, precision=prec)


def _dot_tn(a, b, prec=None):
    return lax.dot_general(a, b, (((0,), (0,)), ((), ())), preferred_element_type=F32, precision=prec)


def _silu(x):
    return x * (1.0 / (1.0 + jnp.exp(-x)))


def _rms(x, g):
    return x * lax.rsqrt(jnp.mean(x * x, axis=-1, keepdims=True) + EPS) * g


def _norm_matmul_kernel(x_ref, g_ref, w_ref, *out_refs, widths):
    xn = _rms(x_ref[...], g_ref[...]).astype(BF16)
    off = 0
    for o_ref, wd in zip(out_refs, widths):
        o_ref[...] = _dot(xn, w_ref[:, off:off + wd])
        off += wd


def _norm_matmul(x, g, w, widths, tm=256):
    n, d = x.shape
    tm = min(tm, n)
    e = w.shape[1]
    assert n % tm == 0 and sum(widths) == e
    return pl.pallas_call(
        functools.partial(_norm_matmul_kernel, widths=widths),
        grid=(n // tm,),
        in_specs=[pl.BlockSpec((tm, d), lambda i: (i, 0)),
                  pl.BlockSpec((1, d), lambda i: (0, 0)),
                  pl.BlockSpec((d, e), lambda i: (0, 0))],
        out_specs=[pl.BlockSpec((tm, wd), lambda i: (i, 0)) for wd in widths],
        out_shape=[jax.ShapeDtypeStruct((n, wd), F32) for wd in widths],
        compiler_params=_params("parallel"),
        name="norm_matmul",
    )(x, g.reshape(1, d), w)


def _out_proj_kernel(*refs, n_in):
    x_refs = refs[:n_in]
    w_ref, g_ref, r_ref, o_ref = refs[n_in:]
    acc, off = None, 0
    for x_ref in x_refs:
        k = x_ref.shape[-1]
        part = _dot(x_ref[...].astype(BF16), w_ref[off:off + k, :])
        acc = part if acc is None else acc + part
        off += k
    o_ref[...] = r_ref[...] + _rms(acc, g_ref[...])


def _out_proj(xs, w, g, resid, tm=256):
    n, d = resid.shape
    tm = min(tm, n)
    return pl.pallas_call(
        functools.partial(_out_proj_kernel, n_in=len(xs)),
        grid=(n // tm,),
        in_specs=[pl.BlockSpec((tm, x.shape[1]), lambda i: (i, 0)) for x in xs]
        + [pl.BlockSpec(w.shape, lambda i: (0, 0)),
           pl.BlockSpec((1, d), lambda i: (0, 0)),
           pl.BlockSpec((tm, d), lambda i: (i, 0))],
        out_specs=pl.BlockSpec((tm, d), lambda i: (i, 0)),
        out_shape=jax.ShapeDtypeStruct((n, d), F32),
        compiler_params=_params("parallel"),
        name="out_proj",
    )(*xs, w, g.reshape(1, d), resid)


def _unit_lower_inverse(a, c):
    r = lax.broadcasted_iota(jnp.int32, (c, c), 0)
    q = lax.broadcasted_iota(jnp.int32, (c, c), 1)
    eye = (r == q).astype(F32)
    same = (r >> 3) == (q >> 3)
    ad = jnp.where(same, a, 0.0)
    a2 = _dot(ad, ad, HI)
    a4 = _dot(a2, a2, HI)
    p1 = eye - ad + a2 - _dot(ad, a2, HI)
    t = p1 + _dot(p1, a4, HI)
    s = 3
    while (1 << s) < c:
        wider = (r >> (s + 1)) == (q >> (s + 1))
        a_off = jnp.where(wider & jnp.logical_not(same), a, 0.0)
        t = t - _dot(_dot(t, a_off, HI), t, HI)
        same = wider
        s += 1
    return t


def _gdn_kernel(u_ref, z_ref, bd_ref, cw_ref, buf0_ref, s0_ref, alog_ref, dtb_ref, gn_ref,
                o_ref, s_out_ref, buf_out_ref, ext_sc, s_sc, *, c, t_last):
    ci = pl.program_id(1)
    nc = pl.num_programs(1)
    head0 = SUBLANES
    keep = GDN_CONV - 1

    @pl.when(ci == 0)
    def _():
        ext_sc[head0 - keep:head0, :] = buf0_ref[0]
        s_sc[...] = s0_ref[0]

    ext_sc[head0:head0 + c, :] = u_ref[0]
    cw = cw_ref[...]
    y = ext_sc[head0:head0 + c, :] * cw[keep:keep + 1]
    for i in range(keep):
        y = y + ext_sc[head0 - keep + i:head0 - keep + i + c, :] * cw[i:i + 1]
    y = _silu(y)

    @pl.when(ci == nc - 1)
    def _():
        buf_out_ref[0] = ext_sc[head0 + t_last - keep:head0 + t_last, :]

    ext_sc[head0 - keep:head0, :] = ext_sc[head0 + c - keep:head0 + c, :]

    bd = bd_ref[0]
    row = lax.broadcasted_iota(jnp.int32, (c, LANES), 0)
    live = row < jnp.where(ci < nc - 1, c, t_last)
    beta_all = jnp.where(live, 1.0 / (1.0 + jnp.exp(-bd)), 0.0)
    xg = bd + dtb_ref[...]
    softplus = jnp.maximum(xg, 0.0) + jnp.log1p(jnp.exp(-jnp.abs(xg)))
    g_all = jnp.where(live, -jnp.exp(alog_ref[...]) * softplus, 0.0)

    r = lax.broadcasted_iota(jnp.int32, (c, c), 0)
    q = lax.broadcasted_iota(jnp.int32, (c, c), 1)
    tri = r >= q
    strict = r > q
    gcum_all = _dot(tri.astype(F32), g_all, HI)
    pick = (lax.broadcasted_iota(jnp.int32, (SUBLANES, LANES), 0) + GDN_HEADS
            == lax.broadcasted_iota(jnp.int32, (SUBLANES, LANES), 1)).astype(F32)
    gcum_rows = _dot_nt(pick, gcum_all, HI)

    for h in range(GDN_HEADS):
        qh = y[:, h * GDN_DK:(h + 1) * GDN_DK]
        kh = y[:, GDN_QK + h * GDN_DK:GDN_QK + (h + 1) * GDN_DK]
        vh = y[:, 2 * GDN_QK + h * GDN_DV:2 * GDN_QK + (h + 1) * GDN_DV]
        qn = qh * lax.rsqrt(jnp.sum(qh * qh, axis=-1, keepdims=True) + EPS) * GDN_DK ** -0.5
        kn = kh * lax.rsqrt(jnp.sum(kh * kh, axis=-1, keepdims=True) + EPS)
        beta = beta_all[:, h:h + 1]
        gc = gcum_all[:, GDN_HEADS + h:GDN_HEADS + h + 1]
        gr = gcum_rows[h:h + 1, :]
        decay = jnp.exp(jnp.where(tri, gc - gr, NEG_INF))
        kbeta = kn * beta
        kn_b = kn.astype(BF16)
        a = jnp.where(strict, _dot_nt(kbeta.astype(BF16), kn_b) * decay, 0.0)
        t_inv = _unit_lower_inverse(a, c)
        rhs = jnp.concatenate([vh * beta, kbeta * jnp.exp(gc)], axis=-1)
        sol = _dot(t_inv, rhs, HI)
        s_old = s_sc[h]
        s_b = s_old.astype(BF16)
        u = sol[:, :GDN_DV] - _dot(sol[:, GDN_DV:].astype(BF16), s_b)
        u_b = u.astype(BF16)
        qk = _dot_nt(qn.astype(BF16), kn_b) * decay
        o = _dot((qn * jnp.exp(gc)).astype(BF16), s_b) + _dot(qk.astype(BF16), u_b)
        g_last = gc[c - 1:c, :]
        kw = kn * jnp.exp(g_last - gc)
        s_sc[h] = s_old * jnp.exp(g_last) + _dot_tn(kw.astype(BF16), u_b)
        zh = z_ref[0, :, h * GDN_DV:(h + 1) * GDN_DV]
        o_ref[0, :, h * GDN_DV:(h + 1) * GDN_DV] = _rms(o, gn_ref[...]) * _silu(zh)

    @pl.when(ci == nc - 1)
    def _():
        s_out_ref[0] = s_sc[...]


def _gdn(u, z, bd, conv_w, buf0, s0, a_log, dt_bias, gdn_norm, t_valid):
    b, tp, _ = u.shape
    c = min(GDN_CHUNK, tp)
    assert tp % c == 0 and c % SUBLANES == 0 and t_valid >= GDN_CONV - 1
    nc = tp // c
    t_last = t_valid - (nc - 1) * c
    assert GDN_CONV - 1 <= t_last <= c
    lane_row = lambda v, off: jnp.zeros((1, LANES), F32).at[0, off:off + GDN_HEADS].set(v)
    keep = GDN_CONV - 1
    return pl.pallas_call(
        functools.partial(_gdn_kernel, c=c, t_last=t_last),
        grid=(b, nc),
        in_specs=[pl.BlockSpec((1, c, GDN_CONV_CH), lambda i, j: (i, j, 0)),
                  pl.BlockSpec((1, c, GDN_VW), lambda i, j: (i, j, 0)),
                  pl.BlockSpec((1, c, LANES), lambda i, j: (i, j, 0)),
                  pl.BlockSpec((GDN_CONV, GDN_CONV_CH), lambda i, j: (0, 0)),
                  pl.BlockSpec((1, keep, GDN_CONV_CH), lambda i, j: (i, 0, 0)),
                  pl.BlockSpec((1, GDN_HEADS, GDN_DK, GDN_DV), lambda i, j: (i, 0, 0, 0)),
                  pl.BlockSpec((1, LANES), lambda i, j: (0, 0)),
                  pl.BlockSpec((1, LANES), lambda i, j: (0, 0)),
                  pl.BlockSpec((1, GDN_DV), lambda i, j: (0, 0))],
        out_specs=[pl.BlockSpec((1, c, GDN_VW), lambda i, j: (i, j, 0)),
                   pl.BlockSpec((1, GDN_HEADS, GDN_DK, GDN_DV), lambda i, j: (i, 0, 0, 0)),
                   pl.BlockSpec((1, keep, GDN_CONV_CH), lambda i, j: (i, 0, 0))],
        out_shape=[jax.ShapeDtypeStruct((b, tp, GDN_VW), F32),
                   jax.ShapeDtypeStruct((b, GDN_HEADS, GDN_DK, GDN_DV), F32),
                   jax.ShapeDtypeStruct((b, keep, GDN_CONV_CH), F32)],
        scratch_shapes=[pltpu.VMEM((SUBLANES + c, GDN_CONV_CH), F32),
                        pltpu.VMEM((GDN_HEADS, GDN_DK, GDN_DV), F32)],
        compiler_params=_params("parallel", "arbitrary"),
        name="gdn",
    )(u, z, bd, conv_w, buf0, s0, lane_row(a_log, GDN_HEADS), lane_row(dt_bias, GDN_HEADS),
      gdn_norm.reshape(1, GDN_DV))


def _moba_select(gate, n_valid):
    rows, nblk = gate.shape
    blk = lax.broadcasted_iota(jnp.int32, (rows, nblk), 1)
    valid = blk < n_valid
    gm = jnp.where(valid, gate, NEG_INF)
    cnt = jnp.zeros((rows, nblk), F32)
    for m in range(nblk):
        gm_m = gm[:, m:m + 1]
        beats = jnp.logical_or(gm_m > gm, jnp.logical_and(gm_m == gm, blk > m))
        cnt = cnt + jnp.where(beats, 1.0, 0.0)
    return jnp.where(jnp.logical_and(cnt < MOBA_TOPK, valid), 1.0, 0.0)


def _block_expand(nblk, length):
    r = lax.broadcasted_iota(jnp.int32, (nblk, length), 0)
    q = lax.broadcasted_iota(jnp.int32, (nblk, length), 1)
    return jnp.where((q >> int(math.log2(MOBA_BLOCK))) == r, 1.0, 0.0).astype(BF16)


def _moba_prompt_kernel(q_ref, k_ref, v_ref, z_ref, o_ref, kmean_sc, *, nblk):
    i = pl.program_id(1)
    t = nblk * MOBA_BLOCK

    @pl.when(i == 0)
    def _():
        kk = k_ref[0].reshape(nblk, MOBA_BLOCK, MOBA_KVW)
        kmean_sc[...] = jnp.sum(kk, axis=1) * (1.0 / MOBA_BLOCK)

    expand = _block_expand(nblk, t)
    rowp = lax.broadcasted_iota(jnp.int32, (MOBA_BLOCK, t), 0) + i * MOBA_BLOCK
    col = lax.broadcasted_iota(jnp.int32, (MOBA_BLOCK, t), 1)
    own_ok = jnp.logical_and(col >= i * MOBA_BLOCK, col <= rowp)
    scale = MOBA_DH ** -0.5
    for h in range(MOBA_HEADS):
        g = h // (MOBA_HEADS // MOBA_KV_HEADS)
        qh = q_ref[0, :, h * MOBA_DH:(h + 1) * MOBA_DH]
        gate = _dot_nt(qh, kmean_sc[:, g * MOBA_DH:(g + 1) * MOBA_DH], HI)
        sel = _moba_select(gate, i)
        sel_cols = _dot(sel.astype(BF16), expand)
        kg = k_ref[0, :, g * MOBA_DH:(g + 1) * MOBA_DH].astype(BF16)
        vg = v_ref[0, :, g * MOBA_DH:(g + 1) * MOBA_DH].astype(BF16)
        s = _dot_nt(qh.astype(BF16), kg) * scale
        s = jnp.where(jnp.logical_or(sel_cols > 0.5, own_ok), s, NEG_INF)
        m = jnp.max(s, axis=-1, keepdims=True)
        p = jnp.exp(s - m)
        l = jnp.sum(p, axis=-1, keepdims=True)
        o = _dot(p.astype(BF16), vg) / l
        o_ref[0, :, h * MOBA_DH:(h + 1) * MOBA_DH] = o * _silu(z_ref[0, :, h * MOBA_DH:(h + 1) * MOBA_DH])


def _moba_prompt(q, k, v, z):
    b, t, _ = q.shape
    assert t % MOBA_BLOCK == 0
    nblk = t // MOBA_BLOCK
    return pl.pallas_call(
        functools.partial(_moba_prompt_kernel, nblk=nblk),
        grid=(b, nblk),
        in_specs=[pl.BlockSpec((1, MOBA_BLOCK, MOBA_QW), lambda i, j: (i, j, 0)),
                  pl.BlockSpec((1, t, MOBA_KVW), lambda i, j: (i, 0, 0)),
                  pl.BlockSpec((1, t, MOBA_KVW), lambda i, j: (i, 0, 0)),
                  pl.BlockSpec((1, MOBA_BLOCK, MOBA_QW), lambda i, j: (i, j, 0))],
        out_specs=pl.BlockSpec((1, MOBA_BLOCK, MOBA_QW), lambda i, j: (i, j, 0)),
        out_shape=jax.ShapeDtypeStruct((b, t, MOBA_QW), F32),
        scratch_shapes=[pltpu.VMEM((nblk, MOBA_KVW), F32)],
        compiler_params=_params("parallel", "arbitrary"),
        name="moba_prompt",
    )(q, k, v, z)


def _page_copy(pool_hbm, layer, page, buf, slot, j, sem):
    return pltpu.make_async_copy(pool_hbm.at[layer, page],
                                 buf.at[slot, pl.ds(j * PAGE_SIZE, PAGE_SIZE), :], sem)


def _paged_pipeline(pt_ref, pools, bufs, sems, layer, n_pages):
    b = pl.program_id(0)
    nb = pl.num_programs(0)
    slot = b % 2

    def start_all(seq, sl):
        def body(j, carry):
            page = pt_ref[seq * n_pages + j]
            for pool, buf, sem in zip(pools, bufs, sems):
                _page_copy(pool, layer, page, buf, sl, j, sem.at[sl]).start()
            return carry
        lax.fori_loop(0, n_pages, body, 0)

    @pl.when(b == 0)
    def _():
        start_all(0, 0)

    @pl.when(b + 1 < nb)
    def _():
        start_all(b + 1, 1 - slot)

    def wait_body(j, carry):
        for pool, buf, sem in zip(pools, bufs, sems):
            _page_copy(pool, layer, 0, buf, slot, j, sem.at[slot]).wait()
        return carry
    lax.fori_loop(0, n_pages, wait_body, 0)
    return slot


def _moba_sample_kernel(pt_ref, q_ref, kn_ref, vn_ref, z_ref, kc_hbm, vc_hbm, o_ref,
                        kbuf, vbuf, ksem, vsem, *, layer, n_pages, ds):
    slot = _paged_pipeline(pt_ref, (kc_hbm, vc_hbm), (kbuf, vbuf), (ksem, vsem), layer, n_pages)
    length = n_pages * PAGE_SIZE
    nblk = length // MOBA_BLOCK
    rows = q_ref.shape[2]
    expand = _block_expand(nblk, length)
    tq = lax.broadcasted_iota(jnp.int32, (rows, SUBLANES), 0) & (SUBLANES - 1)
    cn = lax.broadcasted_iota(jnp.int32, (rows, SUBLANES), 1)
    own_ok = jnp.logical_and(cn <= jnp.minimum(tq, ds - 1), cn < ds)
    scale = MOBA_DH ** -0.5
    for g in range(MOBA_KV_HEADS):
        lanes = slice(g * MOBA_DH, (g + 1) * MOBA_DH)
        kg = kbuf[slot, :, lanes]
        kmean = jnp.sum(kg.reshape(nblk, MOBA_BLOCK, MOBA_DH), axis=1) * (1.0 / MOBA_BLOCK)
        qg = q_ref[0, g]
        sel = _moba_select(_dot_nt(qg, kmean, HI), nblk)
        sel_cols = _dot(sel.astype(BF16), expand)
        q_b = qg.astype(BF16)
        s = jnp.where(sel_cols > 0.5, _dot_nt(q_b, kg.astype(BF16)) * scale, NEG_INF)
        s_own = _dot_nt(q_b, kn_ref[0, :, lanes].astype(BF16)) * scale
        s_own = jnp.where(own_ok, s_own, NEG_INF)
        m = jnp.maximum(jnp.max(s, axis=-1, keepdims=True), jnp.max(s_own, axis=-1, keepdims=True))
        p = jnp.exp(s - m)
        p_own = jnp.exp(s_own - m)
        l = jnp.sum(p, axis=-1, keepdims=True) + jnp.sum(p_own, axis=-1, keepdims=True)
        o = (_dot(p.astype(BF16), vbuf[slot, :, lanes].astype(BF16))
             + _dot(p_own.astype(BF16), vn_ref[0, :, lanes].astype(BF16)))
        o_ref[0, g] = o / l * _silu(z_ref[0, g])


def _group_rows(x, ds):
    db = x.shape[0]
    per = MOBA_HEADS // MOBA_KV_HEADS
    x = x.reshape(db, ds, MOBA_KV_HEADS, per, MOBA_DH).transpose(0, 2, 3, 1, 4)
    x = jnp.pad(x, ((0, 0), (0, 0), (0, 0), (0, SUBLANES - ds), (0, 0)))
    return x.reshape(db, MOBA_KV_HEADS, per * SUBLANES, MOBA_DH)


def _ungroup_rows(x, ds):
    db = x.shape[0]
    per = MOBA_HEADS // MOBA_KV_HEADS
    x = x.reshape(db, MOBA_KV_HEADS, per, SUBLANES, MOBA_DH)[:, :, :, :ds]
    return x.transpose(0, 3, 1, 2, 4).reshape(db, ds, MOBA_QW)


def _moba_sample(q, k_new, v_new, z, cache_k, cache_v, page_table, layer):
    db, ds, _ = q.shape
    n_pages = page_table.shape[1]
    length = n_pages * PAGE_SIZE
    assert length % MOBA_BLOCK == 0 and ds <= SUBLANES and length // MOBA_BLOCK >= MOBA_TOPK
    rows = (MOBA_HEADS // MOBA_KV_HEADS) * SUBLANES
    pad_new = lambda a: jnp.pad(a, ((0, 0), (0, SUBLANES - ds), (0, 0)))
    row_spec = pl.BlockSpec((1, MOBA_KV_HEADS, rows, MOBA_DH), lambda b, pt: (b, 0, 0, 0))
    new_spec = pl.BlockSpec((1, SUBLANES, MOBA_KVW), lambda b, pt: (b, 0, 0))
    out = pl.pallas_call(
        functools.partial(_moba_sample_kernel, layer=layer, n_pages=n_pages, ds=ds),
        grid_spec=pltpu.PrefetchScalarGridSpec(
            num_scalar_prefetch=1,
            grid=(db,),
            in_specs=[row_spec, new_spec, new_spec, row_spec,
                      pl.BlockSpec(memory_space=pl.ANY), pl.BlockSpec(memory_space=pl.ANY)],
            out_specs=row_spec,
            scratch_shapes=[pltpu.VMEM((2, length, MOBA_KVW), F32),
                            pltpu.VMEM((2, length, MOBA_KVW), F32),
                            pltpu.SemaphoreType.DMA((2,)),
                            pltpu.SemaphoreType.DMA((2,))]),
        out_shape=jax.ShapeDtypeStruct((db, MOBA_KV_HEADS, rows, MOBA_DH), F32),
        compiler_params=_params("arbitrary"),
        name="moba_sample",
    )(page_table.reshape(-1), _group_rows(q, ds), pad_new(k_new), pad_new(v_new), _group_rows(z, ds),
      cache_k, cache_v)
    return _ungroup_rows(out, ds)


def _mla_prep_kernel(cq_ref, ckv_ref, kra_ref, krb_ref, cos_ref, sin_ref, qn_ref, kvn_ref, wuq_ref, wukv_ref,
                     qcat_ref, ckv_out_ref, kr_out_ref, kcat_ref):
    cos = cos_ref[...]
    sin = sin_ref[...]
    ckv = _rms(ckv_ref[...], kvn_ref[...])
    ckv_out_ref[...] = ckv
    kr = kra_ref[...] * cos + krb_ref[...] * sin
    kr_out_ref[...] = kr[:, :MLA_ROPE]
    kcat_ref[:, :MLA_KV_RANK] = ckv.astype(BF16)
    kcat_ref[:, MLA_KV_RANK:] = kr.astype(BF16)
    xq = _dot(_rms(cq_ref[...], qn_ref[...]).astype(BF16), wuq_ref[...])
    nope_w = MLA_HEADS * MLA_NOPE
    rope_w = MLA_HEADS * LANES
    for h in range(MLA_HEADS):
        q_nope = xq[:, h * MLA_NOPE:(h + 1) * MLA_NOPE].astype(BF16)
        w_uk = wukv_ref[:, h * (MLA_NOPE + MLA_V):h * (MLA_NOPE + MLA_V) + MLA_NOPE]
        qcat_ref[h, :, :MLA_KV_RANK] = _dot_nt(q_nope, w_uk).astype(BF16)
        ra = xq[:, nope_w + h * LANES:nope_w + (h + 1) * LANES]
        rb = xq[:, nope_w + rope_w + h * LANES:nope_w + rope_w + (h + 1) * LANES]
        qcat_ref[h, :, MLA_KV_RANK:] = (ra * cos + rb * sin).astype(BF16)


def _mla_prep(cq, ckv_raw, kr_a, kr_b, cos, sin, q_norm, kv_norm, w_uq_aug, w_ukv, tm=256):
    n = cq.shape[0]
    tm = min(tm, n)
    p = cos.shape[0]
    assert n % tm == 0 and p % tm == 0
    tok = lambda w: pl.BlockSpec((tm, w), lambda i: (i, 0))
    tab = pl.BlockSpec((tm, LANES), lambda i: (i % (p // tm), 0))
    full = lambda a: pl.BlockSpec(a.shape, lambda i: (0,) * a.ndim)
    qn = q_norm.reshape(1, -1)
    kvn = kv_norm.reshape(1, -1)
    return pl.pallas_call(
        _mla_prep_kernel,
        grid=(n // tm,),
        in_specs=[tok(MLA_Q_RANK), tok(MLA_KV_RANK), tok(LANES), tok(LANES), tab, tab,
                  full(qn), full(kvn), full(w_uq_aug), full(w_ukv)],
        out_specs=[pl.BlockSpec((MLA_HEADS, tm, MLA_KCAT), lambda i: (0, i, 0)),
                   tok(MLA_KV_RANK), tok(MLA_ROPE), tok(MLA_KCAT)],
        out_shape=[jax.ShapeDtypeStruct((MLA_HEADS, n, MLA_KCAT), BF16),
                   jax.ShapeDtypeStruct((n, MLA_KV_RANK), F32),
                   jax.ShapeDtypeStruct((n, MLA_ROPE), F32),
                   jax.ShapeDtypeStruct((n, MLA_KCAT), BF16)],
        compiler_params=_params("parallel"),
        name="mla_prep",
    )(cq, ckv_raw, kr_a, kr_b, cos, sin, qn, kvn, w_uq_aug, w_ukv)


def _mla_prompt_kernel(q_ref, k_ref, o_ref, m_sc, l_sc, acc_sc, *, tq):
    i = pl.program_id(1)
    rows = MLA_HEADS * tq
    scale = (MLA_NOPE + MLA_ROPE) ** -0.5
    q = q_ref[:, 0].reshape(rows, MLA_KCAT)
    m_sc[...] = jnp.full((rows, 1), NEG_INF, F32)
    l_sc[...] = jnp.zeros((rows, 1), F32)
    acc_sc[...] = jnp.zeros((rows, MLA_KV_RANK), F32)

    def step(j, masked):
        kj = k_ref[0, pl.ds(pl.multiple_of(j * tq, tq), tq), :]
        s = _dot_nt(q, kj) * scale
        if masked:
            pos = lax.broadcasted_iota(jnp.int32, (rows, tq), 0) & (tq - 1)
            key = lax.broadcasted_iota(jnp.int32, (rows, tq), 1)
            s = jnp.where(key <= pos, s, NEG_INF)
        m_old = m_sc[...]
        m_new = jnp.maximum(m_old, jnp.max(s, axis=-1, keepdims=True))
        alpha = jnp.exp(m_old - m_new)
        p = jnp.exp(s - m_new)
        l_sc[...] = alpha * l_sc[...] + jnp.sum(p, axis=-1, keepdims=True)
        acc_sc[...] = alpha * acc_sc[...] + _dot(p.astype(BF16), kj[:, :MLA_KV_RANK])
        m_sc[...] = m_new

    def body(j, carry):
        step(j, False)
        return carry
    lax.fori_loop(0, i, body, 0)
    step(i, True)
    o_ref[:, 0] = (acc_sc[...] / l_sc[...]).reshape(MLA_HEADS, tq, MLA_KV_RANK)


def _mla_prompt(q_cat, k_cat, tq=256):
    _, b, t, _ = q_cat.shape
    tq = min(tq, t)
    assert t % tq == 0 and tq & (tq - 1) == 0
    rows = MLA_HEADS * tq
    return pl.pallas_call(
        functools.partial(_mla_prompt_kernel, tq=tq),
        grid=(b, t // tq),
        in_specs=[pl.BlockSpec((MLA_HEADS, 1, tq, MLA_KCAT), lambda i, j: (0, i, j, 0)),
                  pl.BlockSpec((1, t, MLA_KCAT), lambda i, j: (i, 0, 0))],
        out_specs=pl.BlockSpec((MLA_HEADS, 1, tq, MLA_KV_RANK), lambda i, j: (0, i, j, 0)),
        out_shape=jax.ShapeDtypeStruct((MLA_HEADS, b, t, MLA_KV_RANK), F32),
        scratch_shapes=[pltpu.VMEM((rows, 1), F32), pltpu.VMEM((rows, 1), F32),
                        pltpu.VMEM((rows, MLA_KV_RANK), F32)],
        compiler_params=_params("parallel", "parallel"),
        name="mla_prompt",
    )(q_cat, k_cat)


def _mla_sample_kernel(pt_ref, q_ref, kn_ref, cc_hbm, rc_hbm, o_ref, cbuf, rbuf, csem, rsem,
                       *, layer, n_pages, ds):
    slot = _paged_pipeline(pt_ref, (cc_hbm, rc_hbm), (cbuf, rbuf), (csem, rsem), layer, n_pages)
    rows = MLA_HEADS * SUBLANES
    scale = (MLA_NOPE + MLA_ROPE) ** -0.5
    q = q_ref[:, 0].reshape(rows, MLA_KCAT)
    ckv = cbuf[slot].astype(BF16)
    kr = rbuf[slot].astype(BF16)
    s = (_dot_nt(q[:, :MLA_KV_RANK], ckv)
         + _dot_nt(q[:, MLA_KV_RANK:MLA_KV_RANK + MLA_ROPE], kr)) * scale
    kn = kn_ref[0]
    s_own = _dot_nt(q, kn) * scale
    tq = lax.broadcasted_iota(jnp.int32, (rows, SUBLANES), 0) & (SUBLANES - 1)
    cn = lax.broadcasted_iota(jnp.int32, (rows, SUBLANES), 1)
    s_own = jnp.where(jnp.logical_and(cn <= jnp.minimum(tq, ds - 1), cn < ds), s_own, NEG_INF)
    m = jnp.maximum(jnp.max(s, axis=-1, keepdims=True), jnp.max(s_own, axis=-1, keepdims=True))
    p = jnp.exp(s - m)
    p_own = jnp.exp(s_own - m)
    l = jnp.sum(p, axis=-1, keepdims=True) + jnp.sum(p_own, axis=-1, keepdims=True)
    o = _dot(p.astype(BF16), ckv) + _dot(p_own.astype(BF16), kn[:, :MLA_KV_RANK])
    o_ref[:, 0] = (o / l).reshape(MLA_HEADS, SUBLANES, MLA_KV_RANK)


def _mla_sample(q_cat, k_cat_new, cache_ckv, cache_kr, page_table, layer):
    _, db, ds, _ = q_cat.shape
    n_pages = page_table.shape[1]
    length = n_pages * PAGE_SIZE
    assert ds <= SUBLANES
    q_pad = jnp.pad(q_cat, ((0, 0), (0, 0), (0, SUBLANES - ds), (0, 0)))
    k_pad = jnp.pad(k_cat_new, ((0, 0), (0, SUBLANES - ds), (0, 0)))
    out = pl.pallas_call(
        functools.partial(_mla_sample_kernel, layer=layer, n_pages=n_pages, ds=ds),
        grid_spec=pltpu.PrefetchScalarGridSpec(
            num_scalar_prefetch=1,
            grid=(db,),
            in_specs=[pl.BlockSpec((MLA_HEADS, 1, SUBLANES, MLA_KCAT), lambda b, pt: (0, b, 0, 0)),
                      pl.BlockSpec((1, SUBLANES, MLA_KCAT), lambda b, pt: (b, 0, 0)),
                      pl.BlockSpec(memory_space=pl.ANY), pl.BlockSpec(memory_space=pl.ANY)],
            out_specs=pl.BlockSpec((MLA_HEADS, 1, SUBLANES, MLA_KV_RANK), lambda b, pt: (0, b, 0, 0)),
            scratch_shapes=[pltpu.VMEM((2, length, MLA_KV_RANK), F32),
                            pltpu.VMEM((2, length, MLA_ROPE), F32),
                            pltpu.SemaphoreType.DMA((2,)),
                            pltpu.SemaphoreType.DMA((2,))]),
        out_shape=jax.ShapeDtypeStruct((MLA_HEADS, db, SUBLANES, MLA_KV_RANK), F32),
        compiler_params=_params("arbitrary"),
        name="mla_sample",
    )(page_table.reshape(-1), q_pad, k_pad, cache_ckv, cache_kr)
    return out[:, :, :ds]


def _mla_out_kernel(ol_ref, z_ref, wukv_ref, wo_ref, g_ref, r_ref, o_ref):
    acc = None
    for h in range(MLA_HEADS):
        base = h * (MLA_NOPE + MLA_V) + MLA_NOPE
        o_h = _dot(ol_ref[h].astype(BF16), wukv_ref[:, base:base + MLA_V])
        o_h = o_h * _silu(z_ref[:, h * MLA_V:(h + 1) * MLA_V])
        part = _dot(o_h.astype(BF16), wo_ref[h * MLA_V:(h + 1) * MLA_V, :])
        acc = part if acc is None else acc + part
    o_ref[...] = r_ref[...] + _rms(acc, g_ref[...])


def _mla_out(o_lat, z, w_ukv, w_out, g, resid, tm=256):
    n, d = resid.shape
    tm = min(tm, n)
    return pl.pallas_call(
        _mla_out_kernel,
        grid=(n // tm,),
        in_specs=[pl.BlockSpec((MLA_HEADS, tm, MLA_KV_RANK), lambda i: (0, i, 0)),
                  pl.BlockSpec((tm, ODD_MIX), lambda i: (i, 0)),
                  pl.BlockSpec(w_ukv.shape, lambda i: (0, 0)),
                  pl.BlockSpec(w_out.shape, lambda i: (0, 0)),
                  pl.BlockSpec((1, d), lambda i: (0, 0)),
                  pl.BlockSpec((tm, d), lambda i: (i, 0))],
        out_specs=pl.BlockSpec((tm, d), lambda i: (i, 0)),
        out_shape=jax.ShapeDtypeStruct((n, d), F32),
        compiler_params=_params("parallel"),
        name="mla_out",
    )(o_lat, z, w_ukv, w_out, g.reshape(1, d), resid)


EVEN_WIDTHS = (GDN_CONV_CH, GDN_VW, MOBA_QW, MOBA_KVW, MOBA_KVW, MOBA_QW, LANES)
ODD_WIDTHS = (MLA_Q_RANK, MLA_KV_RANK, LANES, LANES, ODD_MIX)


def _even_w_in(w):
    o = 0
    parts = {}
    for name, wd in (("qkv", GDN_CONV_CH), ("za", GDN_VW), ("beta", GDN_HEADS), ("dec", GDN_HEADS),
                     ("qb", MOBA_QW), ("kb", MOBA_KVW), ("vb", MOBA_KVW), ("zb", MOBA_QW)):
        parts[name] = w[:, o:o + wd]
        o += wd
    pad = jnp.zeros((w.shape[0], LANES - 2 * GDN_HEADS), w.dtype)
    return jnp.concatenate([parts["qkv"], parts["za"], parts["qb"], parts["kb"], parts["vb"], parts["zb"],
                            parts["beta"], parts["dec"], pad], axis=1).astype(BF16)


def _rot_half_cols(w):
    half = w.shape[-1] // 2
    return jnp.concatenate([-w[..., half:], w[..., :half]], axis=-1)


def _odd_w_in(w):
    o1 = MLA_Q_RANK + MLA_KV_RANK
    w_kr = w[:, o1:o1 + MLA_ROPE]
    pad = jnp.zeros((w.shape[0], LANES - MLA_ROPE), w.dtype)
    return jnp.concatenate([w[:, :o1], w_kr, pad, _rot_half_cols(w_kr), pad, w[:, o1 + MLA_ROPE:]],
                           axis=1).astype(BF16)


def _odd_w_uq(w):
    r = w.shape[0]
    wh = w.reshape(r, MLA_HEADS, MLA_NOPE + MLA_ROPE)
    nope = wh[..., :MLA_NOPE].reshape(r, MLA_HEADS * MLA_NOPE)
    rope = wh[..., MLA_NOPE:]
    pad = jnp.zeros((r, MLA_HEADS, LANES - MLA_ROPE), w.dtype)
    ra = jnp.concatenate([rope, pad], axis=-1).reshape(r, MLA_HEADS * LANES)
    rb = jnp.concatenate([_rot_half_cols(rope), pad], axis=-1).reshape(r, MLA_HEADS * LANES)
    return jnp.concatenate([nope, ra, rb], axis=1).astype(BF16)


def _rope_tables(pos, reps):
    half = MLA_ROPE // 2
    inv = ROPE_THETA ** (-jnp.arange(half, dtype=F32) / half)
    ang = pos.astype(F32)[:, None] * inv[None, :]
    pad = jnp.zeros((pos.shape[0], LANES - MLA_ROPE), F32)
    cos = jnp.concatenate([jnp.cos(ang), jnp.cos(ang), pad], axis=1)
    sin = jnp.concatenate([jnp.sin(ang), jnp.sin(ang), pad], axis=1)
    return jnp.tile(cos, (reps, 1)), jnp.tile(sin, (reps, 1))


def _even_layer(y, pre_g, post_g, w_in, conv_w, a_log, dt_bias, gdn_norm, w_out, buf0, s0, moba):
    b, t, d = y.shape
    n = b * t
    x2 = y.reshape(n, d)
    qkv, za, qb, kb, vb, zb, bd = _norm_matmul(x2, pre_g, w_in, EVEN_WIDTHS)
    c = min(GDN_CHUNK, -(-t // SUBLANES) * SUBLANES)
    tp = -(-t // c) * c
    seq = lambda a: jnp.pad(a.reshape(b, t, -1), ((0, 0), (0, tp - t), (0, 0)))
    o_a, s_new, buf_new = _gdn(seq(qkv), seq(za), seq(bd), conv_w, buf0, s0, a_log, dt_bias, gdn_norm, t)
    o_a = o_a[:, :t].reshape(n, GDN_VW)
    q3, k3, v3, z3 = (a.reshape(b, t, -1) for a in (qb, kb, vb, zb))
    o_b = moba(q3, k3, v3, z3).reshape(n, MOBA_QW)
    y_new = _out_proj([o_a, o_b], w_out, post_g, x2).reshape(b, t, d)
    kv_shape = (b, t, MOBA_KV_HEADS, MOBA_DH)
    return y_new, s_new, buf_new, kb.reshape(kv_shape), vb.reshape(kv_shape)


def _odd_layer(y, pre_g, post_g, w_in, q_norm, kv_norm, w_uq, w_ukv, w_out, cos, sin, attend):
    b, t, d = y.shape
    n = b * t
    x2 = y.reshape(n, d)
    cq, ckv_raw, kr_a, kr_b, z = _norm_matmul(x2, pre_g, w_in, ODD_WIDTHS)
    q_cat, ckv, krope, k_cat = _mla_prep(cq, ckv_raw, kr_a, kr_b, cos, sin, q_norm, kv_norm, w_uq, w_ukv)
    o_lat = attend(q_cat.reshape(MLA_HEADS, b, t, MLA_KCAT), k_cat.reshape(b, t, MLA_KCAT))
    y_new = _mla_out(o_lat.reshape(MLA_HEADS, n, MLA_KV_RANK), z, w_ukv, w_out, post_g, x2).reshape(b, t, d)
    return y_new, ckv.reshape(b, t, MLA_KV_RANK), krope.reshape(b, t, MLA_ROPE)


def kernel(x_prompt, x_sample, state_gdn, state_conv, cache_moba_k, cache_moba_v, cache_mla_ckv,
           cache_mla_krope, page_table, norm_pre, norm_post, even_w_in, even_conv_w, even_a_log,
           even_dt_bias, even_gdn_norm, even_w_out, odd_w_in, odd_q_norm, odd_kv_norm, odd_w_uq,
           odd_w_ukv, odd_w_out):
    b, t, _ = x_prompt.shape
    db, ds, _ = x_sample.shape
    depth = norm_pre.shape[0]
    past_len = page_table.shape[1] * PAGE_SIZE
    cos_p, sin_p = _rope_tables(jnp.arange(t, dtype=jnp.int32), 1)
    cos_s, sin_s = _rope_tables(past_len + jnp.arange(ds, dtype=jnp.int32), db)
    n_pool = cache_moba_k.shape[1]
    ck = cache_moba_k.reshape(-1, n_pool, PAGE_SIZE, MOBA_KVW)
    cv = cache_moba_v.reshape(-1, n_pool, PAGE_SIZE, MOBA_KVW)

    yp, ys = x_prompt, x_sample
    outs = {k: [] for k in ("p_gdn", "p_conv", "p_k", "p_v", "p_ckv", "p_kr",
                            "s_gdn", "s_conv", "s_k", "s_v", "s_ckv", "s_kr")}
    for l in range(depth):
        i = l // 2
        if l % 2 == 0:
            w = (_even_w_in(even_w_in[i]), even_conv_w[i], even_a_log[i], even_dt_bias[i], even_gdn_norm[i],
                 even_w_out[i].astype(BF16))
            s0 = jnp.zeros((b, GDN_HEADS, GDN_DK, GDN_DV), state_gdn.dtype)
            buf0 = jnp.zeros((b, GDN_CONV - 1, GDN_CONV_CH), x_prompt.dtype)
            yp, st, bf, kk, vv = _even_layer(yp, norm_pre[l], norm_post[l], *w, buf0, s0, _moba_prompt)
            outs["p_gdn"].append(st); outs["p_conv"].append(bf); outs["p_k"].append(kk); outs["p_v"].append(vv)
            moba_s = lambda q, k, v, z, i=i: _moba_sample(q, k, v, z, ck, cv, page_table, i)
            ys, st, bf, kk, vv = _even_layer(ys, norm_pre[l], norm_post[l], *w, state_conv[i], state_gdn[i], moba_s)
            outs["s_gdn"].append(st); outs["s_conv"].append(bf); outs["s_k"].append(kk); outs["s_v"].append(vv)
        else:
            w = (_odd_w_in(odd_w_in[i]), odd_q_norm[i], odd_kv_norm[i], _odd_w_uq(odd_w_uq[i]),
                 odd_w_ukv[i].astype(BF16), odd_w_out[i].astype(BF16))
            yp, c1, r1 = _odd_layer(yp, norm_pre[l], norm_post[l], *w, cos_p, sin_p, _mla_prompt)
            outs["p_ckv"].append(c1); outs["p_kr"].append(r1)
            mla_s = lambda q, k, i=i: _mla_sample(q, k, cache_mla_ckv, cache_mla_krope, page_table, i)
            ys, c1, r1 = _odd_layer(ys, norm_pre[l], norm_post[l], *w, cos_s, sin_s, mla_s)
            outs["s_ckv"].append(c1); outs["s_kr"].append(r1)
    st = lambda k: jnp.stack(outs[k])
    return (yp, ys, st("p_gdn"), st("p_conv"), st("p_k"), st("p_v"), st("p_ckv"), st("p_kr"),
            st("s_gdn"), st("s_conv"), st("s_k"), st("s_v"), st("s_ckv"), st("s_kr"))
```

```python
import functools
import math

import jax
import jax.numpy as jnp
from jax import lax
from jax.experimental import pallas as pl
from jax.experimental.pallas import tpu as pltpu

F32 = jnp.float32
BF16 = jnp.bfloat16
HI = lax.Precision.HIGHEST
EPS = 1e-6
NEG_INF = float("-inf")

LANES = 128
SUBLANES = 8
VMEM_LIMIT = 56 * 1024 * 1024

PAGE_SIZE = 128
DMA_UNROLL = 8
GDN_HEADS = 4
GDN_DK = 128
GDN_DV = 128
GDN_CONV = 4
GDN_CHUNK = 64
GDN_QK = GDN_HEADS * GDN_DK
GDN_VW = GDN_HEADS * GDN_DV
GDN_CONV_CH = 2 * GDN_QK + GDN_VW
MOBA_HEADS = 4
MOBA_KV_HEADS = 2
MOBA_DH = 128
MOBA_BLOCK = 256
MOBA_TOPK = 3
MOBA_QW = MOBA_HEADS * MOBA_DH
MOBA_KVW = MOBA_KV_HEADS * MOBA_DH
MLA_HEADS = 8
MLA_Q_RANK = 384
MLA_KV_RANK = 256
MLA_NOPE = 128
MLA_ROPE = 64
MLA_V = 128
ROPE_THETA = 10000.0
ODD_MIX = MLA_HEADS * MLA_V
MLA_KCAT = MLA_KV_RANK + LANES


def _params(*sem):
    return pltpu.CompilerParams(dimension_semantics=sem, vmem_limit_bytes=VMEM_LIMIT)


def _dot(a, b, prec=None):
    return jnp.dot(a, b, preferred_element_type=F32, precision=prec)


def _dot_nt(a, b, prec=None):
    return lax.dot_general(a, b, (((1,), (1,)), ((), ())), preferred_element_type=F32, precision=prec)


def _dot_tn(a, b, prec=None):
    return lax.dot_general(a, b, (((0,), (0,)), ((), ())), preferred_element_type=F32, precision=prec)


def _split(a):
    hi = a.astype(BF16)
    return hi, (a - hi.astype(F32)).astype(BF16)


def _mm3(a_split, b_split):
    (a_hi, a_lo), (b_hi, b_lo) = a_split, b_split
    return _dot(a_hi, b_hi) + (_dot(a_hi, b_lo) + _dot(a_lo, b_hi))


def _silu(x):
    return x * (1.0 / (1.0 + jnp.exp(-x)))


def _rms(x, g):
    return x * lax.rsqrt(jnp.mean(x * x, axis=-1, keepdims=True) + EPS) * g


def _norm_matmul_kernel(x_ref, g_ref, w_ref, *out_refs, widths):
    xn = _rms(x_ref[...], g_ref[...]).astype(BF16)
    off = 0
    for o_ref, wd in zip(out_refs, widths):
        o_ref[...] = _dot(xn, w_ref[:, off:off + wd])
        off += wd


def _norm_matmul(x, g, w, widths, tm=256):
    n, d = x.shape
    tm = min(tm, n)
    e = w.shape[1]
    assert n % tm == 0 and sum(widths) == e
    return pl.pallas_call(
        functools.partial(_norm_matmul_kernel, widths=widths),
        grid=(n // tm,),
        in_specs=[pl.BlockSpec((tm, d), lambda i: (i, 0)),
                  pl.BlockSpec((1, d), lambda i: (0, 0)),
                  pl.BlockSpec((d, e), lambda i: (0, 0))],
        out_specs=[pl.BlockSpec((tm, wd), lambda i: (i, 0)) for wd in widths],
        out_shape=[jax.ShapeDtypeStruct((n, wd), F32) for wd in widths],
        compiler_params=_params("parallel"),
        name="norm_matmul",
    )(x, g.reshape(1, d), w)


def _out_proj_kernel(*refs, n_in):
    x_refs = refs[:n_in]
    w_ref, g_ref, r_ref, o_ref = refs[n_in:]
    acc, off = None, 0
    for x_ref in x_refs:
        k = x_ref.shape[-1]
        part = _dot(x_ref[...].astype(BF16), w_ref[off:off + k, :])
        acc = part if acc is None else acc + part
        off += k
    o_ref[...] = r_ref[...] + _rms(acc, g_ref[...])


def _out_proj(xs, w, g, resid, tm=256):
    n, d = resid.shape
    tm = min(tm, n)
    return pl.pallas_call(
        functools.partial(_out_proj_kernel, n_in=len(xs)),
        grid=(n // tm,),
        in_specs=[pl.BlockSpec((tm, x.shape[1]), lambda i: (i, 0)) for x in xs]
        + [pl.BlockSpec(w.shape, lambda i: (0, 0)),
           pl.BlockSpec((1, d), lambda i: (0, 0)),
           pl.BlockSpec((tm, d), lambda i: (i, 0))],
        out_specs=pl.BlockSpec((tm, d), lambda i: (i, 0)),
        out_shape=jax.ShapeDtypeStruct((n, d), F32),
        compiler_params=_params("parallel"),
        name="out_proj",
    )(*xs, w, g.reshape(1, d), resid)


def _unit_lower_inverse(a_list, c):
    r = lax.broadcasted_iota(jnp.int32, (c, c), 0)
    q = lax.broadcasted_iota(jnp.int32, (c, c), 1)
    eye = (r == q).astype(F32)
    same = (r >> 3) == (q >> 3)
    ad = [jnp.where(same, a, 0.0) for a in a_list]
    ad_s = [_split(x) for x in ad]
    a2 = [_mm3(x, x) for x in ad_s]
    a2_s = [_split(x) for x in a2]
    a4 = [_mm3(x, x) for x in a2_s]
    p1 = [eye - x + y - _mm3(xs, ys) for x, y, xs, ys in zip(ad, a2, ad_s, a2_s)]
    t = [p + _mm3(_split(p), _split(y)) for p, y in zip(p1, a4)]
    s = 3
    while (1 << s) < c:
        wider = (r >> (s + 1)) == (q >> (s + 1))
        off = jnp.logical_and(wider, jnp.logical_not(same))
        t_s = [_split(x) for x in t]
        ta = [_mm3(xs, _split(jnp.where(off, a, 0.0))) for xs, a in zip(t_s, a_list)]
        t = [x - _mm3(_split(y), xs) for x, y, xs in zip(t, ta, t_s)]
        same = wider
        s += 1
    return t


def _gdn_kernel(u_ref, z_ref, bd_ref, cw_ref, buf0_ref, s0_ref, alog_ref, dtb_ref, gn_ref,
                o_ref, s_out_ref, buf_out_ref, ext_sc, s_sc, *, c, t_last, nb):
    ci = pl.program_id(1)
    nc = pl.num_programs(1)
    head0 = SUBLANES
    keep = GDN_CONV - 1

    @pl.when(ci == 0)
    def _():
        ext_sc[:, head0 - keep:head0, :] = buf0_ref[...]
        s_sc[...] = s0_ref[...]

    cw = cw_ref[...]
    r = lax.broadcasted_iota(jnp.int32, (c, c), 0)
    q = lax.broadcasted_iota(jnp.int32, (c, c), 1)
    tri = r >= q
    strict = r > q
    tri_f = tri.astype(F32)
    pick = (lax.broadcasted_iota(jnp.int32, (SUBLANES, LANES), 0) + GDN_HEADS
            == lax.broadcasted_iota(jnp.int32, (SUBLANES, LANES), 1)).astype(F32)
    row = lax.broadcasted_iota(jnp.int32, (c, LANES), 0)
    live = row < jnp.where(ci < nc - 1, c, t_last)
    neg_a = -jnp.exp(alog_ref[...])

    ys, betas, gcums, grows = [], [], [], []
    for g in range(nb):
        ext_sc[g, head0:head0 + c, :] = u_ref[g]
        y = ext_sc[g, head0:head0 + c, :] * cw[keep:keep + 1]
        for i in range(keep):
            y = y + ext_sc[g, head0 - keep + i:head0 - keep + i + c, :] * cw[i:i + 1]
        ys.append(_silu(y))
        bd = bd_ref[g]
        betas.append(jnp.where(live, 1.0 / (1.0 + jnp.exp(-bd)), 0.0))
        xg = bd + dtb_ref[...]
        softplus = jnp.maximum(xg, 0.0) + jnp.log1p(jnp.exp(-jnp.abs(xg)))
        gcum = _dot(tri_f, jnp.where(live, neg_a * softplus, 0.0), HI)
        gcums.append(gcum)
        grows.append(_dot_nt(pick, gcum, HI))

    @pl.when(ci == nc - 1)
    def _():
        buf_out_ref[...] = ext_sc[:, head0 + t_last - keep:head0 + t_last, :]

    ext_sc[:, head0 - keep:head0, :] = ext_sc[:, head0 + c - keep:head0 + c, :]

    pairs = [(g, h) for g in range(nb) for h in range(GDN_HEADS)]
    l2 = lambda x: x * lax.rsqrt(jnp.sum(x * x, axis=-1, keepdims=True) + EPS)
    qn = [l2(ys[g][:, h * GDN_DK:(h + 1) * GDN_DK]) * GDN_DK ** -0.5 for g, h in pairs]
    kn = [l2(ys[g][:, GDN_QK + h * GDN_DK:GDN_QK + (h + 1) * GDN_DK]) for g, h in pairs]
    vh = [ys[g][:, 2 * GDN_QK + h * GDN_DV:2 * GDN_QK + (h + 1) * GDN_DV] for g, h in pairs]
    beta = [betas[g][:, h:h + 1] for g, h in pairs]
    gc = [gcums[g][:, GDN_HEADS + h:GDN_HEADS + h + 1] for g, h in pairs]
    decay = [jnp.exp(jnp.where(tri, gc[p] - grows[g][h:h + 1, :], NEG_INF)) for p, (g, h) in enumerate(pairs)]
    kbeta = [k * b for k, b in zip(kn, beta)]
    kn_b = [k.astype(BF16) for k in kn]
    a = [jnp.where(strict, _dot_nt(kb.astype(BF16), k) * d, 0.0) for kb, k, d in zip(kbeta, kn_b, decay)]
    t_inv = _unit_lower_inverse(a, c)
    sol = [_mm3(_split(t), _split(jnp.concatenate([v * b, kb * jnp.exp(x)], axis=-1)))
           for t, v, b, kb, x in zip(t_inv, vh, beta, kbeta, gc)]
    qk = [(_dot_nt(x.astype(BF16), k) * d).astype(BF16) for x, k, d in zip(qn, kn_b, decay)]
    s_old = [s_sc[g, h] for g, h in pairs]
    s_b = [s.astype(BF16) for s in s_old]
    u_b = [(x[:, :GDN_DV] - _dot(x[:, GDN_DV:].astype(BF16), s)).astype(BF16) for x, s in zip(sol, s_b)]
    o = [_dot((x * jnp.exp(e)).astype(BF16), s) + _dot(w, u) for x, e, s, w, u in zip(qn, gc, s_b, qk, u_b)]
    for p, (g, h) in enumerate(pairs):
        g_last = gc[p][c - 1:c, :]
        kw = kn[p] * jnp.exp(g_last - gc[p])
        s_sc[g, h] = s_old[p] * jnp.exp(g_last) + _dot_tn(kw.astype(BF16), u_b[p])
        zh = z_ref[g, :, h * GDN_DV:(h + 1) * GDN_DV]
        o_ref[g, :, h * GDN_DV:(h + 1) * GDN_DV] = _rms(o[p], gn_ref[...]) * _silu(zh)

    @pl.when(ci == nc - 1)
    def _():
        s_out_ref[...] = s_sc[...]


def _gdn(u, z, bd, conv_w, buf0, s0, a_log, dt_bias, gdn_norm, t_valid):
    b, tp, _ = u.shape
    c = min(GDN_CHUNK, tp)
    assert tp % c == 0 and c % SUBLANES == 0 and t_valid >= GDN_CONV - 1
    nc = tp // c
    t_last = t_valid - (nc - 1) * c
    assert GDN_CONV - 1 <= t_last <= c
    nb = 2 if c == GDN_CHUNK else 4
    while b % nb:
        nb //= 2
    lane_row = lambda v, off: jnp.zeros((1, LANES), F32).at[0, off:off + GDN_HEADS].set(v)
    keep = GDN_CONV - 1
    return pl.pallas_call(
        functools.partial(_gdn_kernel, c=c, t_last=t_last, nb=nb),
        grid=(b // nb, nc),
        in_specs=[pl.BlockSpec((nb, c, GDN_CONV_CH), lambda i, j: (i, j, 0)),
                  pl.BlockSpec((nb, c, GDN_VW), lambda i, j: (i, j, 0)),
                  pl.BlockSpec((nb, c, LANES), lambda i, j: (i, j, 0)),
                  pl.BlockSpec((GDN_CONV, GDN_CONV_CH), lambda i, j: (0, 0)),
                  pl.BlockSpec((nb, keep, GDN_CONV_CH), lambda i, j: (i, 0, 0)),
                  pl.BlockSpec((nb, GDN_HEADS, GDN_DK, GDN_DV), lambda i, j: (i, 0, 0, 0)),
                  pl.BlockSpec((1, LANES), lambda i, j: (0, 0)),
                  pl.BlockSpec((1, LANES), lambda i, j: (0, 0)),
                  pl.BlockSpec((1, GDN_DV), lambda i, j: (0, 0))],
        out_specs=[pl.BlockSpec((nb, c, GDN_VW), lambda i, j: (i, j, 0)),
                   pl.BlockSpec((nb, GDN_HEADS, GDN_DK, GDN_DV), lambda i, j: (i, 0, 0, 0)),
                   pl.BlockSpec((nb, keep, GDN_CONV_CH), lambda i, j: (i, 0, 0))],
        out_shape=[jax.ShapeDtypeStruct((b, tp, GDN_VW), F32),
                   jax.ShapeDtypeStruct((b, GDN_HEADS, GDN_DK, GDN_DV), F32),
                   jax.ShapeDtypeStruct((b, keep, GDN_CONV_CH), F32)],
        scratch_shapes=[pltpu.VMEM((nb, SUBLANES + c, GDN_CONV_CH), F32),
                        pltpu.VMEM((nb, GDN_HEADS, GDN_DK, GDN_DV), F32)],
        compiler_params=_params("parallel", "arbitrary"),
        name="gdn",
    )(u, z, bd, conv_w, buf0, s0, lane_row(a_log, GDN_HEADS), lane_row(dt_bias, GDN_HEADS),
      gdn_norm.reshape(1, GDN_DV))


def _moba_select(gate, n_valid):
    rows, nblk = gate.shape
    blk = lax.broadcasted_iota(jnp.int32, (rows, nblk), 1)
    valid = blk < n_valid
    gm = jnp.where(valid, gate, NEG_INF)
    cnt = jnp.zeros((rows, nblk), F32)
    for m in range(nblk):
        gm_m = gm[:, m:m + 1]
        beats = jnp.logical_or(gm_m > gm, jnp.logical_and(gm_m == gm, blk > m))
        cnt = cnt + jnp.where(beats, 1.0, 0.0)
    return jnp.where(jnp.logical_and(cnt < MOBA_TOPK, valid), 1.0, 0.0)


def _block_expand(nblk, length):
    r = lax.broadcasted_iota(jnp.int32, (nblk, length), 0)
    q = lax.broadcasted_iota(jnp.int32, (nblk, length), 1)
    return jnp.where((q >> int(math.log2(MOBA_BLOCK))) == r, 1.0, 0.0).astype(BF16)


def _moba_prompt_kernel(q_ref, k_ref, v_ref, z_ref, o_ref, kmean_sc, *, nblk):
    i = pl.program_id(1)
    t = nblk * MOBA_BLOCK

    @pl.when(i == 0)
    def _():
        kk = k_ref[0].reshape(nblk, MOBA_BLOCK, MOBA_KVW)
        kmean_sc[...] = jnp.sum(kk, axis=1) * (1.0 / MOBA_BLOCK)

    expand = _block_expand(nblk, t)
    rowp = lax.broadcasted_iota(jnp.int32, (MOBA_BLOCK, t), 0) + i * MOBA_BLOCK
    col = lax.broadcasted_iota(jnp.int32, (MOBA_BLOCK, t), 1)
    own_ok = jnp.logical_and(col >= i * MOBA_BLOCK, col <= rowp)
    scale = MOBA_DH ** -0.5
    for h in range(MOBA_HEADS):
        g = h // (MOBA_HEADS // MOBA_KV_HEADS)
        qh = q_ref[0, :, h * MOBA_DH:(h + 1) * MOBA_DH]
        gate = _dot_nt(qh, kmean_sc[:, g * MOBA_DH:(g + 1) * MOBA_DH], HI)
        sel = _moba_select(gate, i)
        sel_cols = _dot(sel.astype(BF16), expand)
        kg = k_ref[0, :, g * MOBA_DH:(g + 1) * MOBA_DH].astype(BF16)
        vg = v_ref[0, :, g * MOBA_DH:(g + 1) * MOBA_DH].astype(BF16)
        s = _dot_nt(qh.astype(BF16), kg) * scale
        s = jnp.where(jnp.logical_or(sel_cols > 0.5, own_ok), s, NEG_INF)
        m = jnp.max(s, axis=-1, keepdims=True)
        p = jnp.exp(s - m)
        l = jnp.sum(p, axis=-1, keepdims=True)
        o = _dot(p.astype(BF16), vg) / l
        o_ref[0, :, h * MOBA_DH:(h + 1) * MOBA_DH] = o * _silu(z_ref[0, :, h * MOBA_DH:(h + 1) * MOBA_DH])


def _moba_prompt(q, k, v, z):
    b, t, _ = q.shape
    assert t % MOBA_BLOCK == 0
    nblk = t // MOBA_BLOCK
    return pl.pallas_call(
        functools.partial(_moba_prompt_kernel, nblk=nblk),
        grid=(b, nblk),
        in_specs=[pl.BlockSpec((1, MOBA_BLOCK, MOBA_QW), lambda i, j: (i, j, 0)),
                  pl.BlockSpec((1, t, MOBA_KVW), lambda i, j: (i, 0, 0)),
                  pl.BlockSpec((1, t, MOBA_KVW), lambda i, j: (i, 0, 0)),
                  pl.BlockSpec((1, MOBA_BLOCK, MOBA_QW), lambda i, j: (i, j, 0))],
        out_specs=pl.BlockSpec((1, MOBA_BLOCK, MOBA_QW), lambda i, j: (i, j, 0)),
        out_shape=jax.ShapeDtypeStruct((b, t, MOBA_QW), F32),
        scratch_shapes=[pltpu.VMEM((nblk, MOBA_KVW), F32)],
        compiler_params=_params("parallel", "arbitrary"),
        name="moba_prompt",
    )(q, k, v, z)


def _row_pages(pool_hbm, layer, buf):
    rows = pool_hbm.shape[2]

    def make(page, slot, j, sem):
        return pltpu.make_async_copy(pool_hbm.at[layer, page],
                                     buf.at[slot, pl.ds(pl.multiple_of(j * rows, rows), rows), :], sem)
    return make


def _lane_pages(pool_hbm, layer, buf):
    keys = pool_hbm.shape[3]

    def make(page, slot, j, sem):
        return pltpu.make_async_copy(pool_hbm.at[layer, page],
                                     buf.at[slot, :, pl.ds(pl.multiple_of(j * keys, keys), keys)], sem)
    return make


def _paged_pipeline(pt_ref, copies, sems, n_pages):
    b = pl.program_id(0)
    nb = pl.num_programs(0)
    slot = b % 2

    def start_all(seq, sl):
        def body(j, carry):
            page = pt_ref[seq * n_pages + j]
            for make, sem in zip(copies, sems):
                make(page, sl, j, sem.at[sl]).start()
            return carry
        lax.fori_loop(0, n_pages, body, 0, unroll=math.gcd(n_pages, DMA_UNROLL))

    @pl.when(b == 0)
    def _():
        start_all(0, 0)

    @pl.when(b + 1 < nb)
    def _():
        start_all(b + 1, 1 - slot)

    def wait_body(j, carry):
        for make, sem in zip(copies, sems):
            make(0, slot, j, sem.at[slot]).wait()
        return carry
    lax.fori_loop(0, n_pages, wait_body, 0, unroll=math.gcd(n_pages, DMA_UNROLL))
    return slot


def _moba_sample_kernel(pt_ref, q_ref, kn_ref, vn_ref, z_ref, kc_hbm, vc_hbm, o_ref,
                        kbuf, vbuf, ksem, vsem, *, layer, n_pages, ds):
    slot = _paged_pipeline(pt_ref, (_row_pages(kc_hbm, layer, kbuf), _row_pages(vc_hbm, layer, vbuf)),
                           (ksem, vsem), n_pages)
    length = n_pages * PAGE_SIZE
    nblk = length // MOBA_BLOCK
    rows = q_ref.shape[2]
    expand = _block_expand(nblk, length)
    tq = lax.broadcasted_iota(jnp.int32, (rows, SUBLANES), 0) & (SUBLANES - 1)
    cn = lax.broadcasted_iota(jnp.int32, (rows, SUBLANES), 1)
    own_ok = jnp.logical_and(cn <= jnp.minimum(tq, ds - 1), cn < ds)
    scale = MOBA_DH ** -0.5
    for g in range(MOBA_KV_HEADS):
        lanes = slice(g * MOBA_DH, (g + 1) * MOBA_DH)
        head_rows = pl.ds(g, length, stride=MOBA_KV_HEADS)
        kg = kbuf[slot, head_rows, :]
        kmean = jnp.sum(kg.reshape(nblk, MOBA_BLOCK, MOBA_DH), axis=1) * (1.0 / MOBA_BLOCK)
        qg = q_ref[0, g]
        sel = _moba_select(_dot_nt(qg, kmean, HI), nblk)
        sel_cols = _dot(sel.astype(BF16), expand)
        q_b = qg.astype(BF16)
        s = jnp.where(sel_cols > 0.5, _dot_nt(q_b, kg.astype(BF16)) * scale, NEG_INF)
        s_own = _dot_nt(q_b, kn_ref[0, :, lanes].astype(BF16)) * scale
        s_own = jnp.where(own_ok, s_own, NEG_INF)
        m = jnp.maximum(jnp.max(s, axis=-1, keepdims=True), jnp.max(s_own, axis=-1, keepdims=True))
        p = jnp.exp(s - m)
        p_own = jnp.exp(s_own - m)
        l = jnp.sum(p, axis=-1, keepdims=True) + jnp.sum(p_own, axis=-1, keepdims=True)
        o = (_dot(p.astype(BF16), vbuf[slot, head_rows, :].astype(BF16))
             + _dot(p_own.astype(BF16), vn_ref[0, :, lanes].astype(BF16)))
        o_ref[0, g] = o / l * _silu(z_ref[0, g])


def _group_rows(x, ds):
    db = x.shape[0]
    per = MOBA_HEADS // MOBA_KV_HEADS
    x = x.reshape(db, ds, MOBA_KV_HEADS, per, MOBA_DH).transpose(0, 2, 3, 1, 4)
    x = jnp.pad(x, ((0, 0), (0, 0), (0, 0), (0, SUBLANES - ds), (0, 0)))
    return x.reshape(db, MOBA_KV_HEADS, per * SUBLANES, MOBA_DH)


def _ungroup_rows(x, ds):
    db = x.shape[0]
    per = MOBA_HEADS // MOBA_KV_HEADS
    x = x.reshape(db, MOBA_KV_HEADS, per, SUBLANES, MOBA_DH)[:, :, :, :ds]
    return x.transpose(0, 3, 1, 2, 4).reshape(db, ds, MOBA_QW)


def _moba_sample(q, k_new, v_new, z, cache_k, cache_v, page_table, layer):
    db, ds, _ = q.shape
    n_pages = page_table.shape[1]
    length = n_pages * PAGE_SIZE
    buf_rows = length * MOBA_KV_HEADS
    assert length % MOBA_BLOCK == 0 and ds <= SUBLANES and length // MOBA_BLOCK >= MOBA_TOPK
    rows = (MOBA_HEADS // MOBA_KV_HEADS) * SUBLANES
    pad_new = lambda a: jnp.pad(a, ((0, 0), (0, SUBLANES - ds), (0, 0)))
    row_spec = pl.BlockSpec((1, MOBA_KV_HEADS, rows, MOBA_DH), lambda b, pt: (b, 0, 0, 0))
    new_spec = pl.BlockSpec((1, SUBLANES, MOBA_KVW), lambda b, pt: (b, 0, 0))
    out = pl.pallas_call(
        functools.partial(_moba_sample_kernel, layer=layer, n_pages=n_pages, ds=ds),
        grid_spec=pltpu.PrefetchScalarGridSpec(
            num_scalar_prefetch=1,
            grid=(db,),
            in_specs=[row_spec, new_spec, new_spec, row_spec,
                      pl.BlockSpec(memory_space=pl.ANY), pl.BlockSpec(memory_space=pl.ANY)],
            out_specs=row_spec,
            scratch_shapes=[pltpu.VMEM((2, buf_rows, MOBA_DH), F32),
                            pltpu.VMEM((2, buf_rows, MOBA_DH), F32),
                            pltpu.SemaphoreType.DMA((2,)),
                            pltpu.SemaphoreType.DMA((2,))]),
        out_shape=jax.ShapeDtypeStruct((db, MOBA_KV_HEADS, rows, MOBA_DH), F32),
        compiler_params=_params("arbitrary"),
        name="moba_sample",
    )(page_table.reshape(-1), _group_rows(q, ds), pad_new(k_new), pad_new(v_new), _group_rows(z, ds),
      cache_k, cache_v)
    return _ungroup_rows(out, ds)


def _mla_prep_kernel(cq_ref, ckv_ref, kra_ref, krb_ref, cos_ref, sin_ref, qn_ref, kvn_ref, wuq_ref, wukv_ref,
                     qcat_ref, ckv_out_ref, kr_out_ref, kcat_ref):
    cos = cos_ref[...]
    sin = sin_ref[...]
    ckv = _rms(ckv_ref[...], kvn_ref[...])
    ckv_out_ref[...] = ckv
    kr = kra_ref[...] * cos + krb_ref[...] * sin
    kr_out_ref[...] = kr[:, :MLA_ROPE]
    kcat_ref[:, :MLA_KV_RANK] = ckv.astype(BF16)
    kcat_ref[:, MLA_KV_RANK:] = kr.astype(BF16)
    xq = _dot(_rms(cq_ref[...], qn_ref[...]).astype(BF16), wuq_ref[...])
    nope_w = MLA_HEADS * MLA_NOPE
    rope_w = MLA_HEADS * LANES
    for h in range(MLA_HEADS):
        q_nope = xq[:, h * MLA_NOPE:(h + 1) * MLA_NOPE].astype(BF16)
        w_uk = wukv_ref[:, h * (MLA_NOPE + MLA_V):h * (MLA_NOPE + MLA_V) + MLA_NOPE]
        qcat_ref[h, :, :MLA_KV_RANK] = _dot_nt(q_nope, w_uk).astype(BF16)
        ra = xq[:, nope_w + h * LANES:nope_w + (h + 1) * LANES]
        rb = xq[:, nope_w + rope_w + h * LANES:nope_w + rope_w + (h + 1) * LANES]
        qcat_ref[h, :, MLA_KV_RANK:] = (ra * cos + rb * sin).astype(BF16)


def _mla_prep(cq, ckv_raw, kr_a, kr_b, cos, sin, q_norm, kv_norm, w_uq_aug, w_ukv, tm=256):
    n = cq.shape[0]
    tm = min(tm, n)
    p = cos.shape[0]
    assert n % tm == 0 and p % tm == 0
    tok = lambda w: pl.BlockSpec((tm, w), lambda i: (i, 0))
    tab = pl.BlockSpec((tm, LANES), lambda i: (i % (p // tm), 0))
    full = lambda a: pl.BlockSpec(a.shape, lambda i: (0,) * a.ndim)
    qn = q_norm.reshape(1, -1)
    kvn = kv_norm.reshape(1, -1)
    return pl.pallas_call(
        _mla_prep_kernel,
        grid=(n // tm,),
        in_specs=[tok(MLA_Q_RANK), tok(MLA_KV_RANK), tok(LANES), tok(LANES), tab, tab,
                  full(qn), full(kvn), full(w_uq_aug), full(w_ukv)],
        out_specs=[pl.BlockSpec((MLA_HEADS, tm, MLA_KCAT), lambda i: (0, i, 0)),
                   tok(MLA_KV_RANK), tok(MLA_ROPE), tok(MLA_KCAT)],
        out_shape=[jax.ShapeDtypeStruct((MLA_HEADS, n, MLA_KCAT), BF16),
                   jax.ShapeDtypeStruct((n, MLA_KV_RANK), F32),
                   jax.ShapeDtypeStruct((n, MLA_ROPE), F32),
                   jax.ShapeDtypeStruct((n, MLA_KCAT), BF16)],
        compiler_params=_params("parallel"),
        name="mla_prep",
    )(cq, ckv_raw, kr_a, kr_b, cos, sin, qn, kvn, w_uq_aug, w_ukv)


def _mla_prompt_kernel(q_ref, k_ref, o_ref, mx_sc, mc_sc, l_sc, acc_sc, *, tq):
    i = pl.program_id(1)
    rows = MLA_HEADS * tq
    c = (MLA_NOPE + MLA_ROPE) ** -0.5 * math.log2(math.e)
    q = q_ref[:, 0].reshape(rows, MLA_KCAT)
    pos = lax.broadcasted_iota(jnp.int32, (rows, tq), 0) & (tq - 1)
    key = lax.broadcasted_iota(jnp.int32, (rows, tq), 1)
    causal = key <= pos

    def keys(j):
        return k_ref[0, pl.ds(pl.multiple_of(j * tq, tq), tq), :]

    mx_sc[...] = jnp.where(causal, _dot_nt(q, keys(i)), NEG_INF)

    def sweep_max(j, carry):
        mx_sc[...] = jnp.maximum(mx_sc[...], _dot_nt(q, keys(j)))
        return carry
    lax.fori_loop(0, i, sweep_max, 0)
    mc_sc[...] = jnp.broadcast_to(jnp.max(mx_sc[...], axis=-1, keepdims=True) * c, (rows, LANES))

    def accumulate(j, masked, first):
        kj = keys(j)
        s = _dot_nt(q, kj)
        mc = mc_sc[...]
        p = [jnp.exp2(s[:, w * LANES:(w + 1) * LANES] * c - mc) for w in range(tq // LANES)]
        if masked:
            p = [jnp.where(causal[:, w * LANES:(w + 1) * LANES], x, 0.0) for w, x in enumerate(p)]
        l_new = functools.reduce(lambda x, y: x + y, p)
        pv = _dot(jnp.concatenate(p, axis=-1).astype(BF16), kj[:, :MLA_KV_RANK])
        l_sc[...] = l_new if first else l_sc[...] + l_new
        acc_sc[...] = pv if first else acc_sc[...] + pv

    accumulate(i, True, True)

    def sweep_acc(j, carry):
        accumulate(j, False, False)
        return carry
    lax.fori_loop(0, i, sweep_acc, 0)
    l = jnp.sum(l_sc[...], axis=-1, keepdims=True)
    o_ref[:, 0] = (acc_sc[...] / l).reshape(MLA_HEADS, tq, MLA_KV_RANK)


def _mla_prompt(q_cat, k_cat, tq=256):
    _, b, t, _ = q_cat.shape
    tq = min(tq, t)
    assert t % tq == 0 and tq & (tq - 1) == 0 and tq % LANES == 0
    rows = MLA_HEADS * tq
    return pl.pallas_call(
        functools.partial(_mla_prompt_kernel, tq=tq),
        grid=(b, t // tq),
        in_specs=[pl.BlockSpec((MLA_HEADS, 1, tq, MLA_KCAT), lambda i, j: (0, i, j, 0)),
                  pl.BlockSpec((1, t, MLA_KCAT), lambda i, j: (i, 0, 0))],
        out_specs=pl.BlockSpec((MLA_HEADS, 1, tq, MLA_KV_RANK), lambda i, j: (0, i, j, 0)),
        out_shape=jax.ShapeDtypeStruct((MLA_HEADS, b, t, MLA_KV_RANK), F32),
        scratch_shapes=[pltpu.VMEM((rows, tq), F32), pltpu.VMEM((rows, LANES), F32),
                        pltpu.VMEM((rows, LANES), F32), pltpu.VMEM((rows, MLA_KV_RANK), F32)],
        compiler_params=_params("parallel", "parallel"),
        name="mla_prompt",
    )(q_cat, k_cat)


def _mla_sample_kernel(pt_ref, q_ref, kn_ref, cc_hbm, rc_hbm, o_ref, cbuf, rbuf, csem, rsem,
                       *, layer, n_pages, ds):
    slot = _paged_pipeline(pt_ref, (_row_pages(cc_hbm, layer, cbuf), _lane_pages(rc_hbm, layer, rbuf)),
                           (csem, rsem), n_pages)
    rows = MLA_HEADS * SUBLANES
    scale = (MLA_NOPE + MLA_ROPE) ** -0.5
    q = q_ref[:, 0].reshape(rows, MLA_KCAT)
    ckv = cbuf[slot].astype(BF16)
    kr_t = rbuf[slot].astype(BF16)
    s = (_dot_nt(q[:, :MLA_KV_RANK], ckv)
         + _dot(q[:, MLA_KV_RANK:MLA_KV_RANK + MLA_ROPE], kr_t)) * scale
    kn = kn_ref[0]
    s_own = _dot_nt(q, kn) * scale
    tq = lax.broadcasted_iota(jnp.int32, (rows, SUBLANES), 0) & (SUBLANES - 1)
    cn = lax.broadcasted_iota(jnp.int32, (rows, SUBLANES), 1)
    s_own = jnp.where(jnp.logical_and(cn <= jnp.minimum(tq, ds - 1), cn < ds), s_own, NEG_INF)
    m = jnp.maximum(jnp.max(s, axis=-1, keepdims=True), jnp.max(s_own, axis=-1, keepdims=True))
    p = jnp.exp(s - m)
    p_own = jnp.exp(s_own - m)
    l = jnp.sum(p, axis=-1, keepdims=True) + jnp.sum(p_own, axis=-1, keepdims=True)
    o = _dot(p.astype(BF16), ckv) + _dot(p_own.astype(BF16), kn[:, :MLA_KV_RANK])
    o_ref[:, 0] = (o / l).reshape(MLA_HEADS, SUBLANES, MLA_KV_RANK)


def _mla_sample(q_cat, k_cat_new, cache_ckv, cache_kr, page_table, layer):
    _, db, ds, _ = q_cat.shape
    n_pages = page_table.shape[1]
    length = n_pages * PAGE_SIZE
    assert ds <= SUBLANES
    q_pad = jnp.pad(q_cat, ((0, 0), (0, 0), (0, SUBLANES - ds), (0, 0)))
    k_pad = jnp.pad(k_cat_new, ((0, 0), (0, SUBLANES - ds), (0, 0)))
    out = pl.pallas_call(
        functools.partial(_mla_sample_kernel, layer=layer, n_pages=n_pages, ds=ds),
        grid_spec=pltpu.PrefetchScalarGridSpec(
            num_scalar_prefetch=1,
            grid=(db,),
            in_specs=[pl.BlockSpec((MLA_HEADS, 1, SUBLANES, MLA_KCAT), lambda b, pt: (0, b, 0, 0)),
                      pl.BlockSpec((1, SUBLANES, MLA_KCAT), lambda b, pt: (b, 0, 0)),
                      pl.BlockSpec(memory_space=pl.ANY), pl.BlockSpec(memory_space=pl.ANY)],
            out_specs=pl.BlockSpec((MLA_HEADS, 1, SUBLANES, MLA_KV_RANK), lambda b, pt: (0, b, 0, 0)),
            scratch_shapes=[pltpu.VMEM((2, length, MLA_KV_RANK), F32),
                            pltpu.VMEM((2, MLA_ROPE, length), F32),
                            pltpu.SemaphoreType.DMA((2,)),
                            pltpu.SemaphoreType.DMA((2,))]),
        out_shape=jax.ShapeDtypeStruct((MLA_HEADS, db, SUBLANES, MLA_KV_RANK), F32),
        compiler_params=_params("arbitrary"),
        name="mla_sample",
    )(page_table.reshape(-1), q_pad, k_pad, cache_ckv, cache_kr)
    return out[:, :, :ds]


def _mla_out_kernel(ol_ref, z_ref, wukv_ref, wo_ref, g_ref, r_ref, o_ref):
    acc = None
    for h in range(MLA_HEADS):
        base = h * (MLA_NOPE + MLA_V) + MLA_NOPE
        o_h = _dot(ol_ref[h].astype(BF16), wukv_ref[:, base:base + MLA_V])
        o_h = o_h * _silu(z_ref[:, h * MLA_V:(h + 1) * MLA_V])
        part = _dot(o_h.astype(BF16), wo_ref[h * MLA_V:(h + 1) * MLA_V, :])
        acc = part if acc is None else acc + part
    o_ref[...] = r_ref[...] + _rms(acc, g_ref[...])


def _mla_out(o_lat, z, w_ukv, w_out, g, resid, tm=256):
    n, d = resid.shape
    tm = min(tm, n)
    return pl.pallas_call(
        _mla_out_kernel,
        grid=(n // tm,),
        in_specs=[pl.BlockSpec((MLA_HEADS, tm, MLA_KV_RANK), lambda i: (0, i, 0)),
                  pl.BlockSpec((tm, ODD_MIX), lambda i: (i, 0)),
                  pl.BlockSpec(w_ukv.shape, lambda i: (0, 0)),
                  pl.BlockSpec(w_out.shape, lambda i: (0, 0)),
                  pl.BlockSpec((1, d), lambda i: (0, 0)),
                  pl.BlockSpec((tm, d), lambda i: (i, 0))],
        out_specs=pl.BlockSpec((tm, d), lambda i: (i, 0)),
        out_shape=jax.ShapeDtypeStruct((n, d), F32),
        compiler_params=_params("parallel"),
        name="mla_out",
    )(o_lat, z, w_ukv, w_out, g.reshape(1, d), resid)


EVEN_WIDTHS = (GDN_CONV_CH, GDN_VW, MOBA_QW, MOBA_KVW, MOBA_KVW, MOBA_QW, LANES)
ODD_WIDTHS = (MLA_Q_RANK, MLA_KV_RANK, LANES, LANES, ODD_MIX)


def _even_w_in(w):
    o = 0
    parts = {}
    for name, wd in (("qkv", GDN_CONV_CH), ("za", GDN_VW), ("beta", GDN_HEADS), ("dec", GDN_HEADS),
                     ("qb", MOBA_QW), ("kb", MOBA_KVW), ("vb", MOBA_KVW), ("zb", MOBA_QW)):
        parts[name] = w[:, o:o + wd]
        o += wd
    pad = jnp.zeros((w.shape[0], LANES - 2 * GDN_HEADS), w.dtype)
    return jnp.concatenate([parts["qkv"], parts["za"], parts["qb"], parts["kb"], parts["vb"], parts["zb"],
                            parts["beta"], parts["dec"], pad], axis=1).astype(BF16)


def _rot_half_cols(w):
    half = w.shape[-1] // 2
    return jnp.concatenate([-w[..., half:], w[..., :half]], axis=-1)


def _odd_w_in(w):
    o1 = MLA_Q_RANK + MLA_KV_RANK
    w_kr = w[:, o1:o1 + MLA_ROPE]
    pad = jnp.zeros((w.shape[0], LANES - MLA_ROPE), w.dtype)
    return jnp.concatenate([w[:, :o1], w_kr, pad, _rot_half_cols(w_kr), pad, w[:, o1 + MLA_ROPE:]],
                           axis=1).astype(BF16)


def _odd_w_uq(w):
    r = w.shape[0]
    wh = w.reshape(r, MLA_HEADS, MLA_NOPE + MLA_ROPE)
    nope = wh[..., :MLA_NOPE].reshape(r, MLA_HEADS * MLA_NOPE)
    rope = wh[..., MLA_NOPE:]
    pad = jnp.zeros((r, MLA_HEADS, LANES - MLA_ROPE), w.dtype)
    ra = jnp.concatenate([rope, pad], axis=-1).reshape(r, MLA_HEADS * LANES)
    rb = jnp.concatenate([_rot_half_cols(rope), pad], axis=-1).reshape(r, MLA_HEADS * LANES)
    return jnp.concatenate([nope, ra, rb], axis=1).astype(BF16)


def _rope_tables(pos, reps):
    half = MLA_ROPE // 2
    inv = ROPE_THETA ** (-jnp.arange(half, dtype=F32) / half)
    ang = pos.astype(F32)[:, None] * inv[None, :]
    pad = jnp.zeros((pos.shape[0], LANES - MLA_ROPE), F32)
    cos = jnp.concatenate([jnp.cos(ang), jnp.cos(ang), pad], axis=1)
    sin = jnp.concatenate([jnp.sin(ang), jnp.sin(ang), pad], axis=1)
    return jnp.tile(cos, (reps, 1)), jnp.tile(sin, (reps, 1))


def _even_layer(y, pre_g, post_g, w_in, conv_w, a_log, dt_bias, gdn_norm, w_out, buf0, s0, moba):
    b, t, d = y.shape
    n = b * t
    x2 = y.reshape(n, d)
    qkv, za, qb, kb, vb, zb, bd = _norm_matmul(x2, pre_g, w_in, EVEN_WIDTHS)
    c = min(GDN_CHUNK, -(-t // SUBLANES) * SUBLANES)
    tp = -(-t // c) * c
    seq = lambda a: jnp.pad(a.reshape(b, t, -1), ((0, 0), (0, tp - t), (0, 0)))
    o_a, s_new, buf_new = _gdn(seq(qkv), seq(za), seq(bd), conv_w, buf0, s0, a_log, dt_bias, gdn_norm, t)
    o_a = o_a[:, :t].reshape(n, GDN_VW)
    q3, k3, v3, z3 = (a.reshape(b, t, -1) for a in (qb, kb, vb, zb))
    o_b = moba(q3, k3, v3, z3).reshape(n, MOBA_QW)
    y_new = _out_proj([o_a, o_b], w_out, post_g, x2).reshape(b, t, d)
    kv_shape = (b, t, MOBA_KV_HEADS, MOBA_DH)
    return y_new, s_new, buf_new, kb.reshape(kv_shape), vb.reshape(kv_shape)


def _odd_layer(y, pre_g, post_g, w_in, q_norm, kv_norm, w_uq, w_ukv, w_out, cos, sin, attend):
    b, t, d = y.shape
    n = b * t
    x2 = y.reshape(n, d)
    cq, ckv_raw, kr_a, kr_b, z = _norm_matmul(x2, pre_g, w_in, ODD_WIDTHS)
    q_cat, ckv, krope, k_cat = _mla_prep(cq, ckv_raw, kr_a, kr_b, cos, sin, q_norm, kv_norm, w_uq, w_ukv)
    o_lat = attend(q_cat.reshape(MLA_HEADS, b, t, MLA_KCAT), k_cat.reshape(b, t, MLA_KCAT))
    y_new = _mla_out(o_lat.reshape(MLA_HEADS, n, MLA_KV_RANK), z, w_ukv, w_out, post_g, x2).reshape(b, t, d)
    return y_new, ckv.reshape(b, t, MLA_KV_RANK), krope.reshape(b, t, MLA_ROPE)


def kernel(x_prompt, x_sample, state_gdn, state_conv, cache_moba_k, cache_moba_v, cache_mla_ckv,
           cache_mla_krope, page_table, norm_pre, norm_post, even_w_in, even_conv_w, even_a_log,
           even_dt_bias, even_gdn_norm, even_w_out, odd_w_in, odd_q_norm, odd_kv_norm, odd_w_uq,
           odd_w_ukv, odd_w_out):
    b, t, _ = x_prompt.shape
    db, ds, _ = x_sample.shape
    depth = norm_pre.shape[0]
    past_len = page_table.shape[1] * PAGE_SIZE
    cos_p, sin_p = _rope_tables(jnp.arange(t, dtype=jnp.int32), 1)
    cos_s, sin_s = _rope_tables(past_len + jnp.arange(ds, dtype=jnp.int32), db)
    n_pool = cache_moba_k.shape[1]
    ck = cache_moba_k.reshape(-1, n_pool, PAGE_SIZE * MOBA_KV_HEADS, MOBA_DH)
    cv = cache_moba_v.reshape(-1, n_pool, PAGE_SIZE * MOBA_KV_HEADS, MOBA_DH)
    ckr = jnp.swapaxes(cache_mla_krope, 2, 3)

    yp, ys = x_prompt, x_sample
    outs = {k: [] for k in ("p_gdn", "p_conv", "p_k", "p_v", "p_ckv", "p_kr",
                            "s_gdn", "s_conv", "s_k", "s_v", "s_ckv", "s_kr")}
    for l in range(depth):
        i = l // 2
        if l % 2 == 0:
            w = (_even_w_in(even_w_in[i]), even_conv_w[i], even_a_log[i], even_dt_bias[i], even_gdn_norm[i],
                 even_w_out[i].astype(BF16))
            s0 = jnp.zeros((b, GDN_HEADS, GDN_DK, GDN_DV), state_gdn.dtype)
            buf0 = jnp.zeros((b, GDN_CONV - 1, GDN_CONV_CH), x_prompt.dtype)
            yp, st, bf, kk, vv = _even_layer(yp, norm_pre[l], norm_post[l], *w, buf0, s0, _moba_prompt)
            outs["p_gdn"].append(st); outs["p_conv"].append(bf); outs["p_k"].append(kk); outs["p_v"].append(vv)
            moba_s = lambda q, k, v, z, i=i: _moba_sample(q, k, v, z, ck, cv, page_table, i)
            ys, st, bf, kk, vv = _even_layer(ys, norm_pre[l], norm_post[l], *w, state_conv[i], state_gdn[i], moba_s)
            outs["s_gdn"].append(st); outs["s_conv"].append(bf); outs["s_k"].append(kk); outs["s_v"].append(vv)
        else:
            w = (_odd_w_in(odd_w_in[i]), odd_q_norm[i], odd_kv_norm[i], _odd_w_uq(odd_w_uq[i]),
                 odd_w_ukv[i].astype(BF16), odd_w_out[i].astype(BF16))
            yp, c1, r1 = _odd_layer(yp, norm_pre[l], norm_post[l], *w, cos_p, sin_p, _mla_prompt)
            outs["p_ckv"].append(c1); outs["p_kr"].append(r1)
            mla_s = lambda q, k, i=i: _mla_sample(q, k, cache_mla_ckv, ckr, page_table, i)
            ys, c1, r1 = _odd_layer(ys, norm_pre[l], norm_post[l], *w, cos_s, sin_s, mla_s)
            outs["s_ckv"].append(c1); outs["s_kr"].append(r1)
    st = lambda k: jnp.stack(outs[k])
    return (yp, ys, st("p_gdn"), st("p_conv"), st("p_k"), st("p_v"), st("p_ckv"), st("p_kr"),
            st("s_gdn"), st("s_conv"), st("s_k"), st("s_v"), st("s_ckv"), st("s_kr"))
```

```python
import functools
import math

import jax
import jax.numpy as jnp
from jax import lax
from jax.experimental import pallas as pl
from jax.experimental.pallas import tpu as pltpu

F32 = jnp.float32
BF16 = jnp.bfloat16
HI = lax.Precision.HIGHEST
EPS = 1e-6
NEG_INF = float("-inf")

LANES = 128
SUBLANES = 8
VMEM_LIMIT = 56 * 1024 * 1024

PAGE_SIZE = 128
DMA_UNROLL = 8
GDN_HEADS = 4
GDN_DK = 128
GDN_DV = 128
GDN_CONV = 4
GDN_CHUNK = 64
GDN_QK = GDN_HEADS * GDN_DK
GDN_VW = GDN_HEADS * GDN_DV
GDN_CONV_CH = 2 * GDN_QK + GDN_VW
MOBA_HEADS = 4
MOBA_KV_HEADS = 2
MOBA_DH = 128
MOBA_BLOCK = 256
MOBA_TOPK = 3
MOBA_QW = MOBA_HEADS * MOBA_DH
MOBA_KVW = MOBA_KV_HEADS * MOBA_DH
MLA_HEADS = 8
MLA_Q_RANK = 384
MLA_KV_RANK = 256
MLA_NOPE = 128
MLA_ROPE = 64
MLA_V = 128
ROPE_THETA = 10000.0
ODD_MIX = MLA_HEADS * MLA_V
MLA_KCAT = MLA_KV_RANK + LANES


def _params(*sem):
    return pltpu.CompilerParams(dimension_semantics=sem, vmem_limit_bytes=VMEM_LIMIT)


def _dot(a, b, prec=None):
    return jnp.dot(a, b, preferred_element_type=F32, precision=prec)


def _dot_nt(a, b, prec=None):
    return lax.dot_general(a, b, (((1,), (1,)), ((), ())), preferred_element_type=F32, precision=prec)


def _dot_tn(a, b, prec=None):
    return lax.dot_general(a, b, (((0,), (0,)), ((), ())), preferred_element_type=F32, precision=prec)


def _split(a):
    hi = a.astype(BF16)
    return hi, (a - hi.astype(F32)).astype(BF16)


def _mm3(a_split, b_split):
    (a_hi, a_lo), (b_hi, b_lo) = a_split, b_split
    return _dot(a_hi, b_hi) + (_dot(a_hi, b_lo) + _dot(a_lo, b_hi))


def _silu(x):
    return x * (1.0 / (1.0 + jnp.exp(-x)))


def _rms(x, g):
    return x * lax.rsqrt(jnp.mean(x * x, axis=-1, keepdims=True) + EPS) * g


def _norm_matmul_kernel(x_ref, g_ref, w_ref, *out_refs, widths):
    xn = _rms(x_ref[...], g_ref[...]).astype(BF16)
    off = 0
    for o_ref, wd in zip(out_refs, widths):
        o_ref[...] = _dot(xn, w_ref[:, off:off + wd])
        off += wd


def _norm_matmul(x, g, w, widths, tm=256):
    n, d = x.shape
    tm = min(tm, n)
    e = w.shape[1]
    assert n % tm == 0 and sum(widths) == e
    return pl.pallas_call(
        functools.partial(_norm_matmul_kernel, widths=widths),
        grid=(n // tm,),
        in_specs=[pl.BlockSpec((tm, d), lambda i: (i, 0)),
                  pl.BlockSpec((1, d), lambda i: (0, 0)),
                  pl.BlockSpec((d, e), lambda i: (0, 0))],
        out_specs=[pl.BlockSpec((tm, wd), lambda i: (i, 0)) for wd in widths],
        out_shape=[jax.ShapeDtypeStruct((n, wd), F32) for wd in widths],
        compiler_params=_params("parallel"),
        name="norm_matmul",
    )(x, g.reshape(1, d), w)


def _out_proj_kernel(*refs, n_in):
    x_refs = refs[:n_in]
    w_ref, g_ref, r_ref, o_ref = refs[n_in:]
    acc, off = None, 0
    for x_ref in x_refs:
        k = x_ref.shape[-1]
        part = _dot(x_ref[...].astype(BF16), w_ref[off:off + k, :])
        acc = part if acc is None else acc + part
        off += k
    o_ref[...] = r_ref[...] + _rms(acc, g_ref[...])


def _out_proj(xs, w, g, resid, tm=256):
    n, d = resid.shape
    tm = min(tm, n)
    return pl.pallas_call(
        functools.partial(_out_proj_kernel, n_in=len(xs)),
        grid=(n // tm,),
        in_specs=[pl.BlockSpec((tm, x.shape[1]), lambda i: (i, 0)) for x in xs]
        + [pl.BlockSpec(w.shape, lambda i: (0, 0)),
           pl.BlockSpec((1, d), lambda i: (0, 0)),
           pl.BlockSpec((tm, d), lambda i: (i, 0))],
        out_specs=pl.BlockSpec((tm, d), lambda i: (i, 0)),
        out_shape=jax.ShapeDtypeStruct((n, d), F32),
        compiler_params=_params("parallel"),
        name="out_proj",
    )(*xs, w, g.reshape(1, d), resid)


def _unit_lower_inverse(a_list, c):
    r = lax.broadcasted_iota(jnp.int32, (c, c), 0)
    q = lax.broadcasted_iota(jnp.int32, (c, c), 1)
    eye = (r == q).astype(F32)
    same = (r >> 3) == (q >> 3)
    ad = [jnp.where(same, a, 0.0) for a in a_list]
    ad_s = [_split(x) for x in ad]
    a2 = [_mm3(x, x) for x in ad_s]
    a2_s = [_split(x) for x in a2]
    a4 = [_mm3(x, x) for x in a2_s]
    p1 = [eye - x + y - _mm3(xs, ys) for x, y, xs, ys in zip(ad, a2, ad_s, a2_s)]
    t = [p + _mm3(_split(p), _split(y)) for p, y in zip(p1, a4)]
    s = 3
    while (1 << s) < c:
        wider = (r >> (s + 1)) == (q >> (s + 1))
        off = jnp.logical_and(wider, jnp.logical_not(same))
        t_s = [_split(x) for x in t]
        ta = [_mm3(xs, _split(jnp.where(off, a, 0.0))) for xs, a in zip(t_s, a_list)]
        t = [x - _mm3(_split(y), xs) for x, y, xs in zip(t, ta, t_s)]
        same = wider
        s += 1
    return t


def _gdn_kernel(u_ref, z_ref, bd_ref, cw_ref, buf0_ref, s0_ref, alog_ref, dtb_ref, gn_ref,
                o_ref, s_out_ref, buf_out_ref, ext_sc, s_sc, *, c, t_last, nb):
    ci = pl.program_id(1)
    nc = pl.num_programs(1)
    head0 = SUBLANES
    keep = GDN_CONV - 1

    @pl.when(ci == 0)
    def _():
        ext_sc[:, head0 - keep:head0, :] = buf0_ref[...]
        s_sc[...] = s0_ref[...]

    cw = cw_ref[...]
    r = lax.broadcasted_iota(jnp.int32, (c, c), 0)
    q = lax.broadcasted_iota(jnp.int32, (c, c), 1)
    tri = r >= q
    strict = r > q
    tri_f = tri.astype(F32)
    pick = (lax.broadcasted_iota(jnp.int32, (SUBLANES, LANES), 0) + GDN_HEADS
            == lax.broadcasted_iota(jnp.int32, (SUBLANES, LANES), 1)).astype(F32)
    row = lax.broadcasted_iota(jnp.int32, (c, LANES), 0)
    live = row < jnp.where(ci < nc - 1, c, t_last)
    neg_a = -jnp.exp(alog_ref[...])

    ys, betas, gcums, grows = [], [], [], []
    for g in range(nb):
        ext_sc[g, head0:head0 + c, :] = u_ref[g]
        y = ext_sc[g, head0:head0 + c, :] * cw[keep:keep + 1]
        for i in range(keep):
            y = y + ext_sc[g, head0 - keep + i:head0 - keep + i + c, :] * cw[i:i + 1]
        ys.append(_silu(y))
        bd = bd_ref[g]
        betas.append(jnp.where(live, 1.0 / (1.0 + jnp.exp(-bd)), 0.0))
        xg = bd + dtb_ref[...]
        softplus = jnp.maximum(xg, 0.0) + jnp.log1p(jnp.exp(-jnp.abs(xg)))
        gcum = _dot(tri_f, jnp.where(live, neg_a * softplus, 0.0), HI)
        gcums.append(gcum)
        grows.append(_dot_nt(pick, gcum, HI))

    @pl.when(ci == nc - 1)
    def _():
        buf_out_ref[...] = ext_sc[:, head0 + t_last - keep:head0 + t_last, :]

    ext_sc[:, head0 - keep:head0, :] = ext_sc[:, head0 + c - keep:head0 + c, :]

    pairs = [(g, h) for g in range(nb) for h in range(GDN_HEADS)]
    l2 = lambda x: x * lax.rsqrt(jnp.sum(x * x, axis=-1, keepdims=True) + EPS)
    qn = [l2(ys[g][:, h * GDN_DK:(h + 1) * GDN_DK]) * GDN_DK ** -0.5 for g, h in pairs]
    kn = [l2(ys[g][:, GDN_QK + h * GDN_DK:GDN_QK + (h + 1) * GDN_DK]) for g, h in pairs]
    vh = [ys[g][:, 2 * GDN_QK + h * GDN_DV:2 * GDN_QK + (h + 1) * GDN_DV] for g, h in pairs]
    beta = [betas[g][:, h:h + 1] for g, h in pairs]
    gc = [gcums[g][:, GDN_HEADS + h:GDN_HEADS + h + 1] for g, h in pairs]
    decay = [jnp.exp(jnp.where(tri, gc[p] - grows[g][h:h + 1, :], NEG_INF)) for p, (g, h) in enumerate(pairs)]
    kbeta = [k * b for k, b in zip(kn, beta)]
    kn_b = [k.astype(BF16) for k in kn]
    a = [jnp.where(strict, _dot_nt(kb.astype(BF16), k) * d, 0.0) for kb, k, d in zip(kbeta, kn_b, decay)]
    t_inv = _unit_lower_inverse(a, c)
    sol = [_mm3(_split(t), _split(jnp.concatenate([v * b, kb * jnp.exp(x)], axis=-1)))
           for t, v, b, kb, x in zip(t_inv, vh, beta, kbeta, gc)]
    qk = [(_dot_nt(x.astype(BF16), k) * d).astype(BF16) for x, k, d in zip(qn, kn_b, decay)]
    s_old = [s_sc[g, h] for g, h in pairs]
    s_b = [s.astype(BF16) for s in s_old]
    u_b = [(x[:, :GDN_DV] - _dot(x[:, GDN_DV:].astype(BF16), s)).astype(BF16) for x, s in zip(sol, s_b)]
    o = [_dot((x * jnp.exp(e)).astype(BF16), s) + _dot(w, u) for x, e, s, w, u in zip(qn, gc, s_b, qk, u_b)]
    for p, (g, h) in enumerate(pairs):
        g_last = gc[p][c - 1:c, :]
        kw = kn[p] * jnp.exp(g_last - gc[p])
        s_sc[g, h] = s_old[p] * jnp.exp(g_last) + _dot_tn(kw.astype(BF16), u_b[p])
        zh = z_ref[g, :, h * GDN_DV:(h + 1) * GDN_DV]
        o_ref[g, :, h * GDN_DV:(h + 1) * GDN_DV] = _rms(o[p], gn_ref[...]) * _silu(zh)

    @pl.when(ci == nc - 1)
    def _():
        s_out_ref[...] = s_sc[...]


def _gdn(u, z, bd, conv_w, buf0, s0, a_log, dt_bias, gdn_norm, t_valid):
    b, tp, _ = u.shape
    c = min(GDN_CHUNK, tp)
    assert tp % c == 0 and c % SUBLANES == 0 and t_valid >= GDN_CONV - 1
    nc = tp // c
    t_last = t_valid - (nc - 1) * c
    assert GDN_CONV - 1 <= t_last <= c
    nb = 4
    while b % nb:
        nb //= 2
    lane_row = lambda v, off: jnp.zeros((1, LANES), F32).at[0, off:off + GDN_HEADS].set(v)
    keep = GDN_CONV - 1
    return pl.pallas_call(
        functools.partial(_gdn_kernel, c=c, t_last=t_last, nb=nb),
        grid=(b // nb, nc),
        in_specs=[pl.BlockSpec((nb, c, GDN_CONV_CH), lambda i, j: (i, j, 0)),
                  pl.BlockSpec((nb, c, GDN_VW), lambda i, j: (i, j, 0)),
                  pl.BlockSpec((nb, c, LANES), lambda i, j: (i, j, 0)),
                  pl.BlockSpec((GDN_CONV, GDN_CONV_CH), lambda i, j: (0, 0)),
                  pl.BlockSpec((nb, keep, GDN_CONV_CH), lambda i, j: (i, 0, 0)),
                  pl.BlockSpec((nb, GDN_HEADS, GDN_DK, GDN_DV), lambda i, j: (i, 0, 0, 0)),
                  pl.BlockSpec((1, LANES), lambda i, j: (0, 0)),
                  pl.BlockSpec((1, LANES), lambda i, j: (0, 0)),
                  pl.BlockSpec((1, GDN_DV), lambda i, j: (0, 0))],
        out_specs=[pl.BlockSpec((nb, c, GDN_VW), lambda i, j: (i, j, 0)),
                   pl.BlockSpec((nb, GDN_HEADS, GDN_DK, GDN_DV), lambda i, j: (i, 0, 0, 0)),
                   pl.BlockSpec((nb, keep, GDN_CONV_CH), lambda i, j: (i, 0, 0))],
        out_shape=[jax.ShapeDtypeStruct((b, tp, GDN_VW), F32),
                   jax.ShapeDtypeStruct((b, GDN_HEADS, GDN_DK, GDN_DV), F32),
                   jax.ShapeDtypeStruct((b, keep, GDN_CONV_CH), F32)],
        scratch_shapes=[pltpu.VMEM((nb, SUBLANES + c, GDN_CONV_CH), F32),
                        pltpu.VMEM((nb, GDN_HEADS, GDN_DK, GDN_DV), F32)],
        compiler_params=_params("parallel", "arbitrary"),
        name="gdn",
    )(u, z, bd, conv_w, buf0, s0, lane_row(a_log, GDN_HEADS), lane_row(dt_bias, GDN_HEADS),
      gdn_norm.reshape(1, GDN_DV))


def _moba_select(gate_t, n_valid):
    nblk, nq = gate_t.shape
    blk = lax.broadcasted_iota(jnp.int32, (nblk, nq), 0)
    valid = blk < n_valid
    gm = jnp.where(valid, gate_t, NEG_INF)
    cnt = jnp.zeros((nblk, nq), F32)
    for m in range(nblk):
        gm_m = gm[m:m + 1, :]
        beats = jnp.logical_or(gm_m > gm, jnp.logical_and(gm_m == gm, blk > m))
        cnt = cnt + jnp.where(beats, 1.0, 0.0)
    return jnp.where(jnp.logical_and(cnt < MOBA_TOPK, valid), 1.0, 0.0).astype(BF16)


def _block_expand(nblk, length):
    r = lax.broadcasted_iota(jnp.int32, (nblk, length), 0)
    q = lax.broadcasted_iota(jnp.int32, (nblk, length), 1)
    return jnp.where((q >> int(math.log2(MOBA_BLOCK))) == r, 1.0, 0.0).astype(BF16)


def _moba_prompt_kernel(q_ref, k_ref, v_ref, z_ref, o_ref, kmean_sc, s_sc, mx_sc, mc_sc, l_sc, acc_sc, *, nblk):
    i = pl.program_id(1)
    per = MOBA_HEADS // MOBA_KV_HEADS
    rows = per * MOBA_BLOCK
    groups = range(MOBA_KV_HEADS)
    c = MOBA_DH ** -0.5 * math.log2(math.e)

    @pl.when(i == 0)
    def _():
        kk = k_ref[0].reshape(nblk, MOBA_BLOCK, MOBA_KVW)
        kmean_sc[...] = jnp.sum(kk, axis=1) * (1.0 / MOBA_BLOCK)

    def head_lanes(h):
        return slice(h * MOBA_DH, (h + 1) * MOBA_DH)

    def block(ref, j, g):
        return ref[0, pl.ds(pl.multiple_of(j * MOBA_BLOCK, MOBA_BLOCK), MOBA_BLOCK), head_lanes(g)].astype(BF16)

    qs, sels = [], []
    for g in groups:
        q_g = jnp.concatenate([q_ref[0, :, head_lanes(g * per + hh)] for hh in range(per)], axis=0)
        gate_t = _dot_nt(kmean_sc[:, head_lanes(g)], q_g, HI)
        sels.append(_moba_select(gate_t, i))
        qs.append(q_g.astype(BF16))
    pos = lax.broadcasted_iota(jnp.int32, (rows, MOBA_BLOCK), 0) & (MOBA_BLOCK - 1)
    key = lax.broadcasted_iota(jnp.int32, (rows, MOBA_BLOCK), 1)
    blk_row = lax.broadcasted_iota(jnp.int32, (nblk, MOBA_BLOCK), 0)

    for g in groups:
        s_own = jnp.where(key <= pos, _dot_nt(qs[g], block(k_ref, i, g)), NEG_INF)
        s_sc[i, g] = s_own
        mx_sc[g] = s_own

    def sweep_scores(j, carry):
        pick = jnp.where(blk_row == j, 1.0, 0.0).astype(BF16)
        for g in groups:
            chosen = _dot_tn(sels[g], pick)
            s = jnp.where(chosen > 0.5, _dot_nt(qs[g], block(k_ref, j, g)), NEG_INF)
            s_sc[j, g] = s
            mx_sc[g] = jnp.maximum(mx_sc[g], s)
        return carry
    lax.fori_loop(0, i, sweep_scores, 0)
    for g in groups:
        mc_sc[g] = jnp.broadcast_to(jnp.max(mx_sc[g], axis=-1, keepdims=True) * c, (rows, LANES))

    def accumulate(j, first):
        for g in groups:
            s = s_sc[j, g]
            mc = mc_sc[g]
            p = [jnp.exp2(s[:, w * LANES:(w + 1) * LANES] * c - mc) for w in range(MOBA_BLOCK // LANES)]
            l_new = functools.reduce(lambda x, y: x + y, p)
            pv = _dot(jnp.concatenate(p, axis=-1).astype(BF16), block(v_ref, j, g))
            l_sc[g] = l_new if first else l_sc[g] + l_new
            acc_sc[g] = pv if first else acc_sc[g] + pv

    accumulate(i, True)

    def sweep_acc(j, carry):
        accumulate(j, False)
        return carry
    lax.fori_loop(0, i, sweep_acc, 0)
    for g in groups:
        o = acc_sc[g] / jnp.sum(l_sc[g], axis=-1, keepdims=True)
        for hh in range(per):
            lanes = head_lanes(g * per + hh)
            o_ref[0, :, lanes] = o[hh * MOBA_BLOCK:(hh + 1) * MOBA_BLOCK] * _silu(z_ref[0, :, lanes])


def _moba_prompt(q, k, v, z):
    b, t, _ = q.shape
    assert t % MOBA_BLOCK == 0
    nblk = t // MOBA_BLOCK
    rows = (MOBA_HEADS // MOBA_KV_HEADS) * MOBA_BLOCK
    group = lambda *shape: pltpu.VMEM((MOBA_KV_HEADS, rows) + shape, F32)
    return pl.pallas_call(
        functools.partial(_moba_prompt_kernel, nblk=nblk),
        grid=(b, nblk),
        in_specs=[pl.BlockSpec((1, MOBA_BLOCK, MOBA_QW), lambda i, j: (i, j, 0)),
                  pl.BlockSpec((1, t, MOBA_KVW), lambda i, j: (i, 0, 0)),
                  pl.BlockSpec((1, t, MOBA_KVW), lambda i, j: (i, 0, 0)),
                  pl.BlockSpec((1, MOBA_BLOCK, MOBA_QW), lambda i, j: (i, j, 0))],
        out_specs=pl.BlockSpec((1, MOBA_BLOCK, MOBA_QW), lambda i, j: (i, j, 0)),
        out_shape=jax.ShapeDtypeStruct((b, t, MOBA_QW), F32),
        scratch_shapes=[pltpu.VMEM((nblk, MOBA_KVW), F32),
                        pltpu.VMEM((nblk, MOBA_KV_HEADS, rows, MOBA_BLOCK), F32),
                        group(MOBA_BLOCK), group(LANES), group(LANES), group(MOBA_DH)],
        compiler_params=_params("parallel", "arbitrary"),
        name="moba_prompt",
    )(q, k, v, z)


def _row_pages(pool_hbm, layer, buf):
    rows = pool_hbm.shape[2]

    def make(page, slot, j, sem):
        return pltpu.make_async_copy(pool_hbm.at[layer, page],
                                     buf.at[slot, pl.ds(pl.multiple_of(j * rows, rows), rows), :], sem)
    return make


def _lane_pages(pool_hbm, layer, buf):
    keys = pool_hbm.shape[3]

    def make(page, slot, j, sem):
        return pltpu.make_async_copy(pool_hbm.at[layer, page],
                                     buf.at[slot, :, pl.ds(pl.multiple_of(j * keys, keys), keys)], sem)
    return make


def _paged_pipeline(pt_ref, copies, sems, n_pages):
    b = pl.program_id(0)
    nb = pl.num_programs(0)
    slot = b % 2

    def start_all(seq, sl):
        def body(j, carry):
            page = pt_ref[seq * n_pages + j]
            for make, sem in zip(copies, sems):
                make(page, sl, j, sem.at[sl]).start()
            return carry
        lax.fori_loop(0, n_pages, body, 0, unroll=math.gcd(n_pages, DMA_UNROLL))

    @pl.when(b == 0)
    def _():
        start_all(0, 0)

    @pl.when(b + 1 < nb)
    def _():
        start_all(b + 1, 1 - slot)

    def wait_body(j, carry):
        for make, sem in zip(copies, sems):
            make(0, slot, j, sem.at[slot]).wait()
        return carry
    lax.fori_loop(0, n_pages, wait_body, 0, unroll=math.gcd(n_pages, DMA_UNROLL))
    return slot


def _moba_sample_kernel(pt_ref, q_ref, kn_ref, vn_ref, z_ref, kc_hbm, vc_hbm, o_ref,
                        kbuf, vbuf, ksem, vsem, *, layer, n_pages, ds):
    slot = _paged_pipeline(pt_ref, (_row_pages(kc_hbm, layer, kbuf), _row_pages(vc_hbm, layer, vbuf)),
                           (ksem, vsem), n_pages)
    length = n_pages * PAGE_SIZE
    nblk = length // MOBA_BLOCK
    rows = q_ref.shape[2]
    expand = _block_expand(nblk, length)
    tq = lax.broadcasted_iota(jnp.int32, (rows, SUBLANES), 0) & (SUBLANES - 1)
    cn = lax.broadcasted_iota(jnp.int32, (rows, SUBLANES), 1)
    own_ok = jnp.logical_and(cn <= jnp.minimum(tq, ds - 1), cn < ds)
    scale = MOBA_DH ** -0.5
    for g in range(MOBA_KV_HEADS):
        lanes = slice(g * MOBA_DH, (g + 1) * MOBA_DH)
        head_rows = pl.ds(g, length, stride=MOBA_KV_HEADS)
        kg = kbuf[slot, head_rows, :]
        kmean = jnp.sum(kg.reshape(nblk, MOBA_BLOCK, MOBA_DH), axis=1) * (1.0 / MOBA_BLOCK)
        qg = q_ref[0, g]
        sel = _moba_select(_dot_nt(kmean, qg, HI), nblk)
        sel_cols = _dot_tn(sel, expand)
        q_b = qg.astype(BF16)
        s = jnp.where(sel_cols > 0.5, _dot_nt(q_b, kg.astype(BF16)) * scale, NEG_INF)
        s_own = _dot_nt(q_b, kn_ref[0, :, lanes].astype(BF16)) * scale
        s_own = jnp.where(own_ok, s_own, NEG_INF)
        m = jnp.maximum(jnp.max(s, axis=-1, keepdims=True), jnp.max(s_own, axis=-1, keepdims=True))
        p = jnp.exp(s - m)
        p_own = jnp.exp(s_own - m)
        l = jnp.sum(p, axis=-1, keepdims=True) + jnp.sum(p_own, axis=-1, keepdims=True)
        o = (_dot(p.astype(BF16), vbuf[slot, head_rows, :].astype(BF16))
             + _dot(p_own.astype(BF16), vn_ref[0, :, lanes].astype(BF16)))
        o_ref[0, g] = o / l * _silu(z_ref[0, g])


def _group_rows(x, ds):
    db = x.shape[0]
    per = MOBA_HEADS // MOBA_KV_HEADS
    x = x.reshape(db, ds, MOBA_KV_HEADS, per, MOBA_DH).transpose(0, 2, 3, 1, 4)
    x = jnp.pad(x, ((0, 0), (0, 0), (0, 0), (0, SUBLANES - ds), (0, 0)))
    return x.reshape(db, MOBA_KV_HEADS, per * SUBLANES, MOBA_DH)


def _ungroup_rows(x, ds):
    db = x.shape[0]
    per = MOBA_HEADS // MOBA_KV_HEADS
    x = x.reshape(db, MOBA_KV_HEADS, per, SUBLANES, MOBA_DH)[:, :, :, :ds]
    return x.transpose(0, 3, 1, 2, 4).reshape(db, ds, MOBA_QW)


def _moba_sample(q, k_new, v_new, z, cache_k, cache_v, page_table, layer):
    db, ds, _ = q.shape
    n_pages = page_table.shape[1]
    length = n_pages * PAGE_SIZE
    buf_rows = length * MOBA_KV_HEADS
    assert length % MOBA_BLOCK == 0 and ds <= SUBLANES and length // MOBA_BLOCK >= MOBA_TOPK
    rows = (MOBA_HEADS // MOBA_KV_HEADS) * SUBLANES
    pad_new = lambda a: jnp.pad(a, ((0, 0), (0, SUBLANES - ds), (0, 0)))
    row_spec = pl.BlockSpec((1, MOBA_KV_HEADS, rows, MOBA_DH), lambda b, pt: (b, 0, 0, 0))
    new_spec = pl.BlockSpec((1, SUBLANES, MOBA_KVW), lambda b, pt: (b, 0, 0))
    out = pl.pallas_call(
        functools.partial(_moba_sample_kernel, layer=layer, n_pages=n_pages, ds=ds),
        grid_spec=pltpu.PrefetchScalarGridSpec(
            num_scalar_prefetch=1,
            grid=(db,),
            in_specs=[row_spec, new_spec, new_spec, row_spec,
                      pl.BlockSpec(memory_space=pl.ANY), pl.BlockSpec(memory_space=pl.ANY)],
            out_specs=row_spec,
            scratch_shapes=[pltpu.VMEM((2, buf_rows, MOBA_DH), F32),
                            pltpu.VMEM((2, buf_rows, MOBA_DH), F32),
                            pltpu.SemaphoreType.DMA((2,)),
                            pltpu.SemaphoreType.DMA((2,))]),
        out_shape=jax.ShapeDtypeStruct((db, MOBA_KV_HEADS, rows, MOBA_DH), F32),
        compiler_params=_params("arbitrary"),
        name="moba_sample",
    )(page_table.reshape(-1), _group_rows(q, ds), pad_new(k_new), pad_new(v_new), _group_rows(z, ds),
      cache_k, cache_v)
    return _ungroup_rows(out, ds)


def _mla_prep_kernel(cq_ref, ckv_ref, kra_ref, krb_ref, cos_ref, sin_ref, qn_ref, kvn_ref, wuq_ref, wukv_ref,
                     qcat_ref, ckv_out_ref, kr_out_ref, kcat_ref):
    cos = cos_ref[...]
    sin = sin_ref[...]
    ckv = _rms(ckv_ref[...], kvn_ref[...])
    ckv_out_ref[...] = ckv
    kr = kra_ref[...] * cos + krb_ref[...] * sin
    kr_out_ref[...] = kr[:, :MLA_ROPE]
    kcat_ref[:, :MLA_KV_RANK] = ckv.astype(BF16)
    kcat_ref[:, MLA_KV_RANK:] = kr.astype(BF16)
    xq = _dot(_rms(cq_ref[...], qn_ref[...]).astype(BF16), wuq_ref[...])
    nope_w = MLA_HEADS * MLA_NOPE
    rope_w = MLA_HEADS * LANES
    for h in range(MLA_HEADS):
        q_nope = xq[:, h * MLA_NOPE:(h + 1) * MLA_NOPE].astype(BF16)
        w_uk = wukv_ref[:, h * (MLA_NOPE + MLA_V):h * (MLA_NOPE + MLA_V) + MLA_NOPE]
        qcat_ref[h, :, :MLA_KV_RANK] = _dot_nt(q_nope, w_uk).astype(BF16)
        ra = xq[:, nope_w + h * LANES:nope_w + (h + 1) * LANES]
        rb = xq[:, nope_w + rope_w + h * LANES:nope_w + rope_w + (h + 1) * LANES]
        qcat_ref[h, :, MLA_KV_RANK:] = (ra * cos + rb * sin).astype(BF16)


def _mla_prep(cq, ckv_raw, kr_a, kr_b, cos, sin, q_norm, kv_norm, w_uq_aug, w_ukv, tm=256):
    n = cq.shape[0]
    tm = min(tm, n)
    p = cos.shape[0]
    assert n % tm == 0 and p % tm == 0
    tok = lambda w: pl.BlockSpec((tm, w), lambda i: (i, 0))
    tab = pl.BlockSpec((tm, LANES), lambda i: (i % (p // tm), 0))
    full = lambda a: pl.BlockSpec(a.shape, lambda i: (0,) * a.ndim)
    qn = q_norm.reshape(1, -1)
    kvn = kv_norm.reshape(1, -1)
    return pl.pallas_call(
        _mla_prep_kernel,
        grid=(n // tm,),
        in_specs=[tok(MLA_Q_RANK), tok(MLA_KV_RANK), tok(LANES), tok(LANES), tab, tab,
                  full(qn), full(kvn), full(w_uq_aug), full(w_ukv)],
        out_specs=[pl.BlockSpec((MLA_HEADS, tm, MLA_KCAT), lambda i: (0, i, 0)),
                   tok(MLA_KV_RANK), tok(MLA_ROPE), tok(MLA_KCAT)],
        out_shape=[jax.ShapeDtypeStruct((MLA_HEADS, n, MLA_KCAT), BF16),
                   jax.ShapeDtypeStruct((n, MLA_KV_RANK), F32),
                   jax.ShapeDtypeStruct((n, MLA_ROPE), F32),
                   jax.ShapeDtypeStruct((n, MLA_KCAT), BF16)],
        compiler_params=_params("parallel"),
        name="mla_prep",
    )(cq, ckv_raw, kr_a, kr_b, cos, sin, qn, kvn, w_uq_aug, w_ukv)


def _mla_prompt_kernel(q_ref, k_ref, o_ref, s_sc, mx_sc, mc_sc, l_sc, acc_sc, *, tq):
    i = pl.program_id(1)
    rows = MLA_HEADS * tq
    c = (MLA_NOPE + MLA_ROPE) ** -0.5 * math.log2(math.e)
    q = q_ref[:, 0].reshape(rows, MLA_KCAT)
    pos = lax.broadcasted_iota(jnp.int32, (rows, tq), 0) & (tq - 1)
    key = lax.broadcasted_iota(jnp.int32, (rows, tq), 1)

    def keys(j):
        return k_ref[0, pl.ds(pl.multiple_of(j * tq, tq), tq), :]

    s_own = jnp.where(key <= pos, _dot_nt(q, keys(i)), NEG_INF)
    s_sc[i] = s_own
    mx_sc[...] = s_own

    def sweep_scores(j, carry):
        s = _dot_nt(q, keys(j))
        s_sc[j] = s
        mx_sc[...] = jnp.maximum(mx_sc[...], s)
        return carry
    lax.fori_loop(0, i, sweep_scores, 0)
    mc_sc[...] = jnp.broadcast_to(jnp.max(mx_sc[...], axis=-1, keepdims=True) * c, (rows, LANES))

    def accumulate(j, first):
        s = s_sc[j]
        mc = mc_sc[...]
        p = [jnp.exp2(s[:, w * LANES:(w + 1) * LANES] * c - mc) for w in range(tq // LANES)]
        l_new = functools.reduce(lambda x, y: x + y, p)
        pv = _dot(jnp.concatenate(p, axis=-1).astype(BF16), keys(j)[:, :MLA_KV_RANK])
        l_sc[...] = l_new if first else l_sc[...] + l_new
        acc_sc[...] = pv if first else acc_sc[...] + pv

    accumulate(i, True)

    def sweep_acc(j, carry):
        accumulate(j, False)
        return carry
    lax.fori_loop(0, i, sweep_acc, 0)
    l = jnp.sum(l_sc[...], axis=-1, keepdims=True)
    o_ref[:, 0] = (acc_sc[...] / l).astype(o_ref.dtype).reshape(MLA_HEADS, tq, MLA_KV_RANK)


def _mla_prompt(q_cat, k_cat, tq=256):
    _, b, t, _ = q_cat.shape
    tq = min(tq, t)
    assert t % tq == 0 and tq & (tq - 1) == 0 and tq % LANES == 0
    rows = MLA_HEADS * tq
    return pl.pallas_call(
        functools.partial(_mla_prompt_kernel, tq=tq),
        grid=(b, t // tq),
        in_specs=[pl.BlockSpec((MLA_HEADS, 1, tq, MLA_KCAT), lambda i, j: (0, i, j, 0)),
                  pl.BlockSpec((1, t, MLA_KCAT), lambda i, j: (i, 0, 0))],
        out_specs=pl.BlockSpec((MLA_HEADS, 1, tq, MLA_KV_RANK), lambda i, j: (0, i, j, 0)),
        out_shape=jax.ShapeDtypeStruct((MLA_HEADS, b, t, MLA_KV_RANK), BF16),
        scratch_shapes=[pltpu.VMEM((t // tq, rows, tq), F32),
                        pltpu.VMEM((rows, tq), F32), pltpu.VMEM((rows, LANES), F32),
                        pltpu.VMEM((rows, LANES), F32), pltpu.VMEM((rows, MLA_KV_RANK), F32)],
        compiler_params=_params("parallel", "parallel"),
        name="mla_prompt",
    )(q_cat, k_cat)


def _mla_sample_kernel(pt_ref, q_ref, kn_ref, cc_hbm, rc_hbm, o_ref, cbuf, rbuf, csem, rsem,
                       *, layer, n_pages, ds):
    slot = _paged_pipeline(pt_ref, (_row_pages(cc_hbm, layer, cbuf), _lane_pages(rc_hbm, layer, rbuf)),
                           (csem, rsem), n_pages)
    rows = MLA_HEADS * SUBLANES
    scale = (MLA_NOPE + MLA_ROPE) ** -0.5
    q = q_ref[:, 0].reshape(rows, MLA_KCAT)
    q_lat = q[:, :MLA_KV_RANK]
    q_rope = q[:, MLA_KV_RANK:MLA_KV_RANK + MLA_ROPE]
    kn = kn_ref[0]
    s_own = _dot_nt(q, kn) * scale
    tq = lax.broadcasted_iota(jnp.int32, (rows, SUBLANES), 0) & (SUBLANES - 1)
    cn = lax.broadcasted_iota(jnp.int32, (rows, SUBLANES), 1)
    s_own = jnp.where(jnp.logical_and(cn <= jnp.minimum(tq, ds - 1), cn < ds), s_own, NEG_INF)
    ckv = cbuf[slot].astype(BF16)
    kr_t = rbuf[slot].astype(BF16)
    s = (_dot_nt(q_lat, ckv) + _dot(q_rope, kr_t)) * scale
    m = jnp.maximum(jnp.max(s, axis=-1, keepdims=True), jnp.max(s_own, axis=-1, keepdims=True))
    p = jnp.exp(s - m)
    p_own = jnp.exp(s_own - m)
    l = jnp.sum(p, axis=-1, keepdims=True) + jnp.sum(p_own, axis=-1, keepdims=True)
    o = _dot(p.astype(BF16), ckv) + _dot(p_own.astype(BF16), kn[:, :MLA_KV_RANK])
    o_ref[:, 0] = (o / l).reshape(MLA_HEADS, SUBLANES, MLA_KV_RANK)


def _mla_sample(q_cat, k_cat_new, cache_ckv, cache_kr, page_table, layer):
    _, db, ds, _ = q_cat.shape
    n_pages = page_table.shape[1]
    length = n_pages * PAGE_SIZE
    assert ds <= SUBLANES
    q_pad = jnp.pad(q_cat, ((0, 0), (0, 0), (0, SUBLANES - ds), (0, 0)))
    k_pad = jnp.pad(k_cat_new, ((0, 0), (0, SUBLANES - ds), (0, 0)))
    out = pl.pallas_call(
        functools.partial(_mla_sample_kernel, layer=layer, n_pages=n_pages, ds=ds),
        grid_spec=pltpu.PrefetchScalarGridSpec(
            num_scalar_prefetch=1,
            grid=(db,),
            in_specs=[pl.BlockSpec((MLA_HEADS, 1, SUBLANES, MLA_KCAT), lambda b, pt: (0, b, 0, 0)),
                      pl.BlockSpec((1, SUBLANES, MLA_KCAT), lambda b, pt: (b, 0, 0)),
                      pl.BlockSpec(memory_space=pl.ANY), pl.BlockSpec(memory_space=pl.ANY)],
            out_specs=pl.BlockSpec((MLA_HEADS, 1, SUBLANES, MLA_KV_RANK), lambda b, pt: (0, b, 0, 0)),
            scratch_shapes=[pltpu.VMEM((2, length, MLA_KV_RANK), F32),
                            pltpu.VMEM((2, MLA_ROPE, length), F32),
                            pltpu.SemaphoreType.DMA((2,)),
                            pltpu.SemaphoreType.DMA((2,))]),
        out_shape=jax.ShapeDtypeStruct((MLA_HEADS, db, SUBLANES, MLA_KV_RANK), F32),
        compiler_params=_params("arbitrary"),
        name="mla_sample",
    )(page_table.reshape(-1), q_pad, k_pad, cache_ckv, cache_kr)
    return out[:, :, :ds]


def _mla_out_kernel(ol_ref, z_ref, wukv_ref, wo_ref, g_ref, r_ref, o_ref):
    acc = None
    for h in range(MLA_HEADS):
        base = h * (MLA_NOPE + MLA_V) + MLA_NOPE
        o_h = _dot(ol_ref[h].astype(BF16), wukv_ref[:, base:base + MLA_V])
        o_h = o_h * _silu(z_ref[:, h * MLA_V:(h + 1) * MLA_V])
        part = _dot(o_h.astype(BF16), wo_ref[h * MLA_V:(h + 1) * MLA_V, :])
        acc = part if acc is None else acc + part
    o_ref[...] = r_ref[...] + _rms(acc, g_ref[...])


def _mla_out(o_lat, z, w_ukv, w_out, g, resid, tm=256):
    n, d = resid.shape
    tm = min(tm, n)
    return pl.pallas_call(
        _mla_out_kernel,
        grid=(n // tm,),
        in_specs=[pl.BlockSpec((MLA_HEADS, tm, MLA_KV_RANK), lambda i: (0, i, 0)),
                  pl.BlockSpec((tm, ODD_MIX), lambda i: (i, 0)),
                  pl.BlockSpec(w_ukv.shape, lambda i: (0, 0)),
                  pl.BlockSpec(w_out.shape, lambda i: (0, 0)),
                  pl.BlockSpec((1, d), lambda i: (0, 0)),
                  pl.BlockSpec((tm, d), lambda i: (i, 0))],
        out_specs=pl.BlockSpec((tm, d), lambda i: (i, 0)),
        out_shape=jax.ShapeDtypeStruct((n, d), F32),
        compiler_params=_params("parallel"),
        name="mla_out",
    )(o_lat, z, w_ukv, w_out, g.reshape(1, d), resid)


EVEN_WIDTHS = (GDN_CONV_CH, GDN_VW, MOBA_QW, MOBA_KVW, MOBA_KVW, MOBA_QW, LANES)
ODD_WIDTHS = (MLA_Q_RANK, MLA_KV_RANK, LANES, LANES, ODD_MIX)


def _even_w_in(w):
    o = 0
    parts = {}
    for name, wd in (("qkv", GDN_CONV_CH), ("za", GDN_VW), ("beta", GDN_HEADS), ("dec", GDN_HEADS),
                     ("qb", MOBA_QW), ("kb", MOBA_KVW), ("vb", MOBA_KVW), ("zb", MOBA_QW)):
        parts[name] = w[:, o:o + wd]
        o += wd
    pad = jnp.zeros((w.shape[0], LANES - 2 * GDN_HEADS), w.dtype)
    return jnp.concatenate([parts["qkv"], parts["za"], parts["qb"], parts["kb"], parts["vb"], parts["zb"],
                            parts["beta"], parts["dec"], pad], axis=1).astype(BF16)


def _rot_half_cols(w):
    half = w.shape[-1] // 2
    return jnp.concatenate([-w[..., half:], w[..., :half]], axis=-1)


def _odd_w_in(w):
    o1 = MLA_Q_RANK + MLA_KV_RANK
    w_kr = w[:, o1:o1 + MLA_ROPE]
    pad = jnp.zeros((w.shape[0], LANES - MLA_ROPE), w.dtype)
    return jnp.concatenate([w[:, :o1], w_kr, pad, _rot_half_cols(w_kr), pad, w[:, o1 + MLA_ROPE:]],
                           axis=1).astype(BF16)


def _odd_w_uq(w):
    r = w.shape[0]
    wh = w.reshape(r, MLA_HEADS, MLA_NOPE + MLA_ROPE)
    nope = wh[..., :MLA_NOPE].reshape(r, MLA_HEADS * MLA_NOPE)
    rope = wh[..., MLA_NOPE:]
    pad = jnp.zeros((r, MLA_HEADS, LANES - MLA_ROPE), w.dtype)
    ra = jnp.concatenate([rope, pad], axis=-1).reshape(r, MLA_HEADS * LANES)
    rb = jnp.concatenate([_rot_half_cols(rope), pad], axis=-1).reshape(r, MLA_HEADS * LANES)
    return jnp.concatenate([nope, ra, rb], axis=1).astype(BF16)


def _rope_tables(pos, reps):
    half = MLA_ROPE // 2
    inv = ROPE_THETA ** (-jnp.arange(half, dtype=F32) / half)
    ang = pos.astype(F32)[:, None] * inv[None, :]
    pad = jnp.zeros((pos.shape[0], LANES - MLA_ROPE), F32)
    cos = jnp.concatenate([jnp.cos(ang), jnp.cos(ang), pad], axis=1)
    sin = jnp.concatenate([jnp.sin(ang), jnp.sin(ang), pad], axis=1)
    return jnp.tile(cos, (reps, 1)), jnp.tile(sin, (reps, 1))


def _even_layer(y, pre_g, post_g, w_in, conv_w, a_log, dt_bias, gdn_norm, w_out, buf0, s0, moba):
    b, t, d = y.shape
    n = b * t
    x2 = y.reshape(n, d)
    qkv, za, qb, kb, vb, zb, bd = _norm_matmul(x2, pre_g, w_in, EVEN_WIDTHS)
    c = min(GDN_CHUNK, -(-t // SUBLANES) * SUBLANES)
    tp = -(-t // c) * c
    seq = lambda a: jnp.pad(a.reshape(b, t, -1), ((0, 0), (0, tp - t), (0, 0)))
    o_a, s_new, buf_new = _gdn(seq(qkv), seq(za), seq(bd), conv_w, buf0, s0, a_log, dt_bias, gdn_norm, t)
    o_a = o_a[:, :t].reshape(n, GDN_VW)
    q3, k3, v3, z3 = (a.reshape(b, t, -1) for a in (qb, kb, vb, zb))
    o_b = moba(q3, k3, v3, z3).reshape(n, MOBA_QW)
    y_new = _out_proj([o_a, o_b], w_out, post_g, x2).reshape(b, t, d)
    kv_shape = (b, t, MOBA_KV_HEADS, MOBA_DH)
    return y_new, s_new, buf_new, kb.reshape(kv_shape), vb.reshape(kv_shape)


def _odd_layer(y, pre_g, post_g, w_in, q_norm, kv_norm, w_uq, w_ukv, w_out, cos, sin, attend):
    b, t, d = y.shape
    n = b * t
    x2 = y.reshape(n, d)
    cq, ckv_raw, kr_a, kr_b, z = _norm_matmul(x2, pre_g, w_in, ODD_WIDTHS)
    q_cat, ckv, krope, k_cat = _mla_prep(cq, ckv_raw, kr_a, kr_b, cos, sin, q_norm, kv_norm, w_uq, w_ukv)
    o_lat = attend(q_cat.reshape(MLA_HEADS, b, t, MLA_KCAT), k_cat.reshape(b, t, MLA_KCAT))
    y_new = _mla_out(o_lat.reshape(MLA_HEADS, n, MLA_KV_RANK), z, w_ukv, w_out, post_g, x2).reshape(b, t, d)
    return y_new, ckv.reshape(b, t, MLA_KV_RANK), krope.reshape(b, t, MLA_ROPE)


def kernel(x_prompt, x_sample, state_gdn, state_conv, cache_moba_k, cache_moba_v, cache_mla_ckv,
           cache_mla_krope, page_table, norm_pre, norm_post, even_w_in, even_conv_w, even_a_log,
           even_dt_bias, even_gdn_norm, even_w_out, odd_w_in, odd_q_norm, odd_kv_norm, odd_w_uq,
           odd_w_ukv, odd_w_out):
    b, t, _ = x_prompt.shape
    db, ds, _ = x_sample.shape
    depth = norm_pre.shape[0]
    past_len = page_table.shape[1] * PAGE_SIZE
    cos_p, sin_p = _rope_tables(jnp.arange(t, dtype=jnp.int32), 1)
    cos_s, sin_s = _rope_tables(past_len + jnp.arange(ds, dtype=jnp.int32), db)
    n_pool = cache_moba_k.shape[1]
    ck = cache_moba_k.reshape(-1, n_pool, PAGE_SIZE * MOBA_KV_HEADS, MOBA_DH)
    cv = cache_moba_v.reshape(-1, n_pool, PAGE_SIZE * MOBA_KV_HEADS, MOBA_DH)
    ckr = jnp.swapaxes(cache_mla_krope, 2, 3)

    yp, ys = x_prompt, x_sample
    outs = {k: [] for k in ("p_gdn", "p_conv", "p_k", "p_v", "p_ckv", "p_kr",
                            "s_gdn", "s_conv", "s_k", "s_v", "s_ckv", "s_kr")}
    for l in range(depth):
        i = l // 2
        if l % 2 == 0:
            w = (_even_w_in(even_w_in[i]), even_conv_w[i], even_a_log[i], even_dt_bias[i], even_gdn_norm[i],
                 even_w_out[i].astype(BF16))
            s0 = jnp.zeros((b, GDN_HEADS, GDN_DK, GDN_DV), state_gdn.dtype)
            buf0 = jnp.zeros((b, GDN_CONV - 1, GDN_CONV_CH), x_prompt.dtype)
            yp, st, bf, kk, vv = _even_layer(yp, norm_pre[l], norm_post[l], *w, buf0, s0, _moba_prompt)
            outs["p_gdn"].append(st); outs["p_conv"].append(bf); outs["p_k"].append(kk); outs["p_v"].append(vv)
            moba_s = lambda q, k, v, z, i=i: _moba_sample(q, k, v, z, ck, cv, page_table, i)
            ys, st, bf, kk, vv = _even_layer(ys, norm_pre[l], norm_post[l], *w, state_conv[i], state_gdn[i], moba_s)
            outs["s_gdn"].append(st); outs["s_conv"].append(bf); outs["s_k"].append(kk); outs["s_v"].append(vv)
        else:
            w = (_odd_w_in(odd_w_in[i]), odd_q_norm[i], odd_kv_norm[i], _odd_w_uq(odd_w_uq[i]),
                 odd_w_ukv[i].astype(BF16), odd_w_out[i].astype(BF16))
            yp, c1, r1 = _odd_layer(yp, norm_pre[l], norm_post[l], *w, cos_p, sin_p, _mla_prompt)
            outs["p_ckv"].append(c1); outs["p_kr"].append(r1)
            mla_s = lambda q, k, i=i: _mla_sample(q, k, cache_mla_ckv, ckr, page_table, i)
            ys, c1, r1 = _odd_layer(ys, norm_pre[l], norm_post[l], *w, cos_s, sin_s, mla_s)
            outs["s_ckv"].append(c1); outs["s_kr"].append(r1)
    st = lambda k: jnp.stack(outs[k])
    return (yp, ys, st("p_gdn"), st("p_conv"), st("p_k"), st("p_v"), st("p_ckv"), st("p_kr"),
            st("s_gdn"), st("s_conv"), st("s_k"), st("s_v"), st("s_ckv"), st("s_kr"))
```

```python
import functools
import math

import jax
import jax.numpy as jnp
from jax import lax
from jax.experimental import pallas as pl
from jax.experimental.pallas import tpu as pltpu

F32 = jnp.float32
BF16 = jnp.bfloat16
HI = lax.Precision.HIGHEST
EPS = 1e-6
NEG_INF = float("-inf")

LANES = 128
SUBLANES = 8
VMEM_LIMIT = 56 * 1024 * 1024

PAGE_SIZE = 128
DMA_UNROLL = 8
TOKEN_TILE = 512
GDN_HEADS = 4
GDN_DK = 128
GDN_DV = 128
GDN_CONV = 4
GDN_CHUNK = 64
GDN_QK = GDN_HEADS * GDN_DK
GDN_VW = GDN_HEADS * GDN_DV
GDN_CONV_CH = 2 * GDN_QK + GDN_VW
MOBA_HEADS = 4
MOBA_KV_HEADS = 2
MOBA_DH = 128
MOBA_BLOCK = 256
MOBA_TOPK = 3
MOBA_QW = MOBA_HEADS * MOBA_DH
MOBA_KVW = MOBA_KV_HEADS * MOBA_DH
MLA_HEADS = 8
MLA_Q_RANK = 384
MLA_KV_RANK = 256
MLA_NOPE = 128
MLA_ROPE = 64
MLA_V = 128
ROPE_THETA = 10000.0
ODD_MIX = MLA_HEADS * MLA_V
MLA_KCAT = MLA_KV_RANK + LANES


def _params(*sem):
    return pltpu.CompilerParams(dimension_semantics=sem, vmem_limit_bytes=VMEM_LIMIT)


def _dot(a, b, prec=None):
    return jnp.dot(a, b, preferred_element_type=F32, precision=prec)


def _dot_nt(a, b, prec=None):
    return lax.dot_general(a, b, (((1,), (1,)), ((), ())), preferred_element_type=F32, precision=prec)


def _dot_tn(a, b, prec=None):
    return lax.dot_general(a, b, (((0,), (0,)), ((), ())), preferred_element_type=F32, precision=prec)


def _split(a):
    hi = a.astype(BF16)
    return hi, (a - hi.astype(F32)).astype(BF16)


def _mm3(a_split, b_split):
    (a_hi, a_lo), (b_hi, b_lo) = a_split, b_split
    return _dot(a_hi, b_hi) + (_dot(a_hi, b_lo) + _dot(a_lo, b_hi))


def _silu(x):
    return x * (1.0 / (1.0 + jnp.exp(-x)))


def _rms(x, g):
    return x * lax.rsqrt(jnp.mean(x * x, axis=-1, keepdims=True) + EPS) * g


def _norm_matmul_kernel(x_ref, g_ref, w_ref, *out_refs, widths):
    xn = _rms(x_ref[...], g_ref[...]).astype(BF16)
    off = 0
    for o_ref, wd in zip(out_refs, widths):
        o_ref[...] = _dot(xn, w_ref[:, off:off + wd])
        off += wd


def _norm_matmul(x, g, w, widths, tm=TOKEN_TILE):
    n, d = x.shape
    tm = min(tm, n)
    e = w.shape[1]
    assert n % tm == 0 and sum(widths) == e
    return pl.pallas_call(
        functools.partial(_norm_matmul_kernel, widths=widths),
        grid=(n // tm,),
        in_specs=[pl.BlockSpec((tm, d), lambda i: (i, 0)),
                  pl.BlockSpec((1, d), lambda i: (0, 0)),
                  pl.BlockSpec((d, e), lambda i: (0, 0))],
        out_specs=[pl.BlockSpec((tm, wd), lambda i: (i, 0)) for wd in widths],
        out_shape=[jax.ShapeDtypeStruct((n, wd), F32) for wd in widths],
        compiler_params=_params("parallel"),
        name="norm_matmul",
    )(x, g.reshape(1, d), w)


def _out_proj_kernel(*refs, n_in):
    x_refs = refs[:n_in]
    w_ref, g_ref, r_ref, o_ref = refs[n_in:]
    acc, off = None, 0
    for x_ref in x_refs:
        k = x_ref.shape[-1]
        part = _dot(x_ref[...].astype(BF16), w_ref[off:off + k, :])
        acc = part if acc is None else acc + part
        off += k
    o_ref[...] = r_ref[...] + _rms(acc, g_ref[...])


def _out_proj(xs, w, g, resid, tm=TOKEN_TILE):
    n, d = resid.shape
    tm = min(tm, n)
    return pl.pallas_call(
        functools.partial(_out_proj_kernel, n_in=len(xs)),
        grid=(n // tm,),
        in_specs=[pl.BlockSpec((tm, x.shape[1]), lambda i: (i, 0)) for x in xs]
        + [pl.BlockSpec(w.shape, lambda i: (0, 0)),
           pl.BlockSpec((1, d), lambda i: (0, 0)),
           pl.BlockSpec((tm, d), lambda i: (i, 0))],
        out_specs=pl.BlockSpec((tm, d), lambda i: (i, 0)),
        out_shape=jax.ShapeDtypeStruct((n, d), F32),
        compiler_params=_params("parallel"),
        name="out_proj",
    )(*xs, w, g.reshape(1, d), resid)


def _unit_lower_inverse(a_list, c):
    r = lax.broadcasted_iota(jnp.int32, (c, c), 0)
    q = lax.broadcasted_iota(jnp.int32, (c, c), 1)
    eye = (r == q).astype(F32)
    same = (r >> 3) == (q >> 3)
    ad = [jnp.where(same, a, 0.0) for a in a_list]
    ad_s = [_split(x) for x in ad]
    a2 = [_mm3(x, x) for x in ad_s]
    a2_s = [_split(x) for x in a2]
    a4 = [_mm3(x, x) for x in a2_s]
    p1 = [eye - x + y - _mm3(xs, ys) for x, y, xs, ys in zip(ad, a2, ad_s, a2_s)]
    t = [p + _mm3(_split(p), _split(y)) for p, y in zip(p1, a4)]
    s = 3
    while (1 << s) < c:
        wider = (r >> (s + 1)) == (q >> (s + 1))
        off = jnp.logical_and(wider, jnp.logical_not(same))
        t_s = [_split(x) for x in t]
        ta = [_mm3(xs, _split(jnp.where(off, a, 0.0))) for xs, a in zip(t_s, a_list)]
        t = [x - _mm3(_split(y), xs) for x, y, xs in zip(t, ta, t_s)]
        same = wider
        s += 1
    return t


def _gdn_kernel(u_ref, z_ref, bd_ref, cw_ref, buf0_ref, s0_ref, alog_ref, dtb_ref, gn_ref,
                o_ref, s_out_ref, buf_out_ref, ext_sc, s_sc, *, c, t_last, nb):
    ci = pl.program_id(1)
    nc = pl.num_programs(1)
    head0 = SUBLANES
    keep = GDN_CONV - 1

    @pl.when(ci == 0)
    def _():
        ext_sc[:, head0 - keep:head0, :] = buf0_ref[...]
        s_sc[...] = s0_ref[...]

    cw = cw_ref[...]
    r = lax.broadcasted_iota(jnp.int32, (c, c), 0)
    q = lax.broadcasted_iota(jnp.int32, (c, c), 1)
    tri = r >= q
    strict = r > q
    tri_f = tri.astype(F32)
    pick = (lax.broadcasted_iota(jnp.int32, (SUBLANES, LANES), 0) + GDN_HEADS
            == lax.broadcasted_iota(jnp.int32, (SUBLANES, LANES), 1)).astype(F32)
    row = lax.broadcasted_iota(jnp.int32, (c, LANES), 0)
    live = row < jnp.where(ci < nc - 1, c, t_last)
    neg_a = -jnp.exp(alog_ref[...])

    ys, betas, gcums, grows = [], [], [], []
    for g in range(nb):
        ext_sc[g, head0:head0 + c, :] = u_ref[g]
        y = ext_sc[g, head0:head0 + c, :] * cw[keep:keep + 1]
        for i in range(keep):
            y = y + ext_sc[g, head0 - keep + i:head0 - keep + i + c, :] * cw[i:i + 1]
        ys.append(_silu(y))
        bd = bd_ref[g]
        betas.append(jnp.where(live, 1.0 / (1.0 + jnp.exp(-bd)), 0.0))
        xg = bd + dtb_ref[...]
        softplus = jnp.maximum(xg, 0.0) + jnp.log1p(jnp.exp(-jnp.abs(xg)))
        gcum = _dot(tri_f, jnp.where(live, neg_a * softplus, 0.0), HI)
        gcums.append(gcum)
        grows.append(_dot_nt(pick, gcum, HI))

    @pl.when(ci == nc - 1)
    def _():
        buf_out_ref[...] = ext_sc[:, head0 + t_last - keep:head0 + t_last, :]

    ext_sc[:, head0 - keep:head0, :] = ext_sc[:, head0 + c - keep:head0 + c, :]

    pairs = [(g, h) for g in range(nb) for h in range(GDN_HEADS)]
    l2 = lambda x: x * lax.rsqrt(jnp.sum(x * x, axis=-1, keepdims=True) + EPS)
    qn = [l2(ys[g][:, h * GDN_DK:(h + 1) * GDN_DK]) * GDN_DK ** -0.5 for g, h in pairs]
    kn = [l2(ys[g][:, GDN_QK + h * GDN_DK:GDN_QK + (h + 1) * GDN_DK]) for g, h in pairs]
    vh = [ys[g][:, 2 * GDN_QK + h * GDN_DV:2 * GDN_QK + (h + 1) * GDN_DV] for g, h in pairs]
    beta = [betas[g][:, h:h + 1] for g, h in pairs]
    gc = [gcums[g][:, GDN_HEADS + h:GDN_HEADS + h + 1] for g, h in pairs]
    decay = [jnp.exp(jnp.where(tri, gc[p] - grows[g][h:h + 1, :], NEG_INF)) for p, (g, h) in enumerate(pairs)]
    kbeta = [k * b for k, b in zip(kn, beta)]
    kn_b = [k.astype(BF16) for k in kn]
    a = [jnp.where(strict, _dot_nt(kb.astype(BF16), k) * d, 0.0) for kb, k, d in zip(kbeta, kn_b, decay)]
    t_inv = _unit_lower_inverse(a, c)
    sol = [_mm3(_split(t), _split(jnp.concatenate([v * b, kb * jnp.exp(x)], axis=-1)))
           for t, v, b, kb, x in zip(t_inv, vh, beta, kbeta, gc)]
    qk = [(_dot_nt(x.astype(BF16), k) * d).astype(BF16) for x, k, d in zip(qn, kn_b, decay)]
    s_old = [s_sc[g, h] for g, h in pairs]
    s_b = [s.astype(BF16) for s in s_old]
    u_b = [(x[:, :GDN_DV] - _dot(x[:, GDN_DV:].astype(BF16), s)).astype(BF16) for x, s in zip(sol, s_b)]
    o = [_dot((x * jnp.exp(e)).astype(BF16), s) + _dot(w, u) for x, e, s, w, u in zip(qn, gc, s_b, qk, u_b)]
    for p, (g, h) in enumerate(pairs):
        g_last = gc[p][c - 1:c, :]
        kw = kn[p] * jnp.exp(g_last - gc[p])
        s_sc[g, h] = s_old[p] * jnp.exp(g_last) + _dot_tn(kw.astype(BF16), u_b[p])
        zh = z_ref[g, :, h * GDN_DV:(h + 1) * GDN_DV]
        o_ref[g, :, h * GDN_DV:(h + 1) * GDN_DV] = _rms(o[p], gn_ref[...]) * _silu(zh)

    @pl.when(ci == nc - 1)
    def _():
        s_out_ref[...] = s_sc[...]


def _gdn(u, z, bd, conv_w, buf0, s0, layer, a_log, dt_bias, gdn_norm, t_valid):
    b, tp, _ = u.shape
    c = min(GDN_CHUNK, tp)
    assert tp % c == 0 and c % SUBLANES == 0 and t_valid >= GDN_CONV - 1
    nc = tp // c
    t_last = t_valid - (nc - 1) * c
    assert GDN_CONV - 1 <= t_last <= c
    nb = 4
    while b % nb:
        nb //= 2
    lane_row = lambda v, off: jnp.zeros((1, LANES), F32).at[0, off:off + GDN_HEADS].set(v)
    keep = GDN_CONV - 1
    return pl.pallas_call(
        functools.partial(_gdn_kernel, c=c, t_last=t_last, nb=nb),
        grid=(b // nb, nc),
        in_specs=[pl.BlockSpec((nb, c, GDN_CONV_CH), lambda i, j: (i, j, 0)),
                  pl.BlockSpec((nb, c, GDN_VW), lambda i, j: (i, j, 0)),
                  pl.BlockSpec((nb, c, LANES), lambda i, j: (i, j, 0)),
                  pl.BlockSpec((GDN_CONV, GDN_CONV_CH), lambda i, j: (0, 0)),
                  pl.BlockSpec((None, nb, keep, GDN_CONV_CH), lambda i, j: (layer, i, 0, 0)),
                  pl.BlockSpec((None, nb, GDN_HEADS, GDN_DK, GDN_DV), lambda i, j: (layer, i, 0, 0, 0)),
                  pl.BlockSpec((1, LANES), lambda i, j: (0, 0)),
                  pl.BlockSpec((1, LANES), lambda i, j: (0, 0)),
                  pl.BlockSpec((1, GDN_DV), lambda i, j: (0, 0))],
        out_specs=[pl.BlockSpec((nb, c, GDN_VW), lambda i, j: (i, j, 0)),
                   pl.BlockSpec((nb, GDN_HEADS, GDN_DK, GDN_DV), lambda i, j: (i, 0, 0, 0)),
                   pl.BlockSpec((nb, keep, GDN_CONV_CH), lambda i, j: (i, 0, 0))],
        out_shape=[jax.ShapeDtypeStruct((b, tp, GDN_VW), F32),
                   jax.ShapeDtypeStruct((b, GDN_HEADS, GDN_DK, GDN_DV), F32),
                   jax.ShapeDtypeStruct((b, keep, GDN_CONV_CH), F32)],
        scratch_shapes=[pltpu.VMEM((nb, SUBLANES + c, GDN_CONV_CH), F32),
                        pltpu.VMEM((nb, GDN_HEADS, GDN_DK, GDN_DV), F32)],
        compiler_params=_params("parallel", "arbitrary"),
        name="gdn",
    )(u, z, bd, conv_w, buf0, s0, lane_row(a_log, GDN_HEADS), lane_row(dt_bias, GDN_HEADS),
      gdn_norm.reshape(1, GDN_DV))


def _moba_select(gate_t, n_valid):
    nblk, nq = gate_t.shape
    blk = lax.broadcasted_iota(jnp.int32, (nblk, nq), 0)
    valid = blk < n_valid
    gm = jnp.where(valid, gate_t, NEG_INF)
    cnt = jnp.zeros((nblk, nq), F32)
    for m in range(nblk):
        gm_m = gm[m:m + 1, :]
        beats = jnp.logical_or(gm_m > gm, jnp.logical_and(gm_m == gm, blk > m))
        cnt = cnt + jnp.where(beats, 1.0, 0.0)
    return jnp.where(jnp.logical_and(cnt < MOBA_TOPK, valid), 1.0, 0.0).astype(BF16)


def _block_expand(nblk, length):
    r = lax.broadcasted_iota(jnp.int32, (nblk, length), 0)
    q = lax.broadcasted_iota(jnp.int32, (nblk, length), 1)
    return jnp.where((q >> int(math.log2(MOBA_BLOCK))) == r, 1.0, 0.0).astype(BF16)


def _moba_prompt_kernel(q_ref, k_ref, v_ref, z_ref, o_ref, kmean_sc, s_sc, mx_sc, mc_sc, l_sc, acc_sc, *, nblk):
    i = pl.program_id(1)
    per = MOBA_HEADS // MOBA_KV_HEADS
    rows = per * MOBA_BLOCK
    groups = range(MOBA_KV_HEADS)
    c = MOBA_DH ** -0.5 * math.log2(math.e)

    @pl.when(i == 0)
    def _():
        kk = k_ref[0].reshape(nblk, MOBA_BLOCK, MOBA_KVW)
        kmean_sc[...] = jnp.sum(kk, axis=1) * (1.0 / MOBA_BLOCK)

    def head_lanes(h):
        return slice(h * MOBA_DH, (h + 1) * MOBA_DH)

    def block(ref, j, g):
        return ref[0, pl.ds(pl.multiple_of(j * MOBA_BLOCK, MOBA_BLOCK), MOBA_BLOCK), head_lanes(g)].astype(BF16)

    qs, sels = [], []
    for g in groups:
        q_g = jnp.concatenate([q_ref[0, :, head_lanes(g * per + hh)] for hh in range(per)], axis=0)
        gate_t = _dot_nt(kmean_sc[:, head_lanes(g)], q_g, HI)
        sels.append(_moba_select(gate_t, i))
        qs.append(q_g.astype(BF16))
    pos = lax.broadcasted_iota(jnp.int32, (rows, MOBA_BLOCK), 0) & (MOBA_BLOCK - 1)
    key = lax.broadcasted_iota(jnp.int32, (rows, MOBA_BLOCK), 1)
    blk_row = lax.broadcasted_iota(jnp.int32, (nblk, MOBA_BLOCK), 0)

    for g in groups:
        s_own = jnp.where(key <= pos, _dot_nt(qs[g], block(k_ref, i, g)), NEG_INF)
        s_sc[i, g] = s_own
        mx_sc[g] = s_own

    def sweep_scores(j, carry):
        pick = jnp.where(blk_row == j, 1.0, 0.0).astype(BF16)
        for g in groups:
            chosen = _dot_tn(sels[g], pick)
            s = jnp.where(chosen > 0.5, _dot_nt(qs[g], block(k_ref, j, g)), NEG_INF)
            s_sc[j, g] = s
            mx_sc[g] = jnp.maximum(mx_sc[g], s)
        return carry
    lax.fori_loop(0, i, sweep_scores, 0)
    for g in groups:
        mc_sc[g] = jnp.broadcast_to(jnp.max(mx_sc[g], axis=-1, keepdims=True) * c, (rows, LANES))

    def accumulate(j, first):
        for g in groups:
            s = s_sc[j, g]
            mc = mc_sc[g]
            p = [jnp.exp2(s[:, w * LANES:(w + 1) * LANES] * c - mc) for w in range(MOBA_BLOCK // LANES)]
            l_new = functools.reduce(lambda x, y: x + y, p)
            pv = _dot(jnp.concatenate(p, axis=-1).astype(BF16), block(v_ref, j, g))
            l_sc[g] = l_new if first else l_sc[g] + l_new
            acc_sc[g] = pv if first else acc_sc[g] + pv

    accumulate(i, True)

    def sweep_acc(j, carry):
        accumulate(j, False)
        return carry
    lax.fori_loop(0, i, sweep_acc, 0)
    for g in groups:
        o = acc_sc[g] / jnp.sum(l_sc[g], axis=-1, keepdims=True)
        for hh in range(per):
            lanes = head_lanes(g * per + hh)
            o_ref[0, :, lanes] = o[hh * MOBA_BLOCK:(hh + 1) * MOBA_BLOCK] * _silu(z_ref[0, :, lanes])


def _moba_prompt(q, k, v, z):
    b, t, _ = q.shape
    assert t % MOBA_BLOCK == 0
    nblk = t // MOBA_BLOCK
    rows = (MOBA_HEADS // MOBA_KV_HEADS) * MOBA_BLOCK
    group = lambda *shape: pltpu.VMEM((MOBA_KV_HEADS, rows) + shape, F32)
    return pl.pallas_call(
        functools.partial(_moba_prompt_kernel, nblk=nblk),
        grid=(b, nblk),
        in_specs=[pl.BlockSpec((1, MOBA_BLOCK, MOBA_QW), lambda i, j: (i, j, 0)),
                  pl.BlockSpec((1, t, MOBA_KVW), lambda i, j: (i, 0, 0)),
                  pl.BlockSpec((1, t, MOBA_KVW), lambda i, j: (i, 0, 0)),
                  pl.BlockSpec((1, MOBA_BLOCK, MOBA_QW), lambda i, j: (i, j, 0))],
        out_specs=pl.BlockSpec((1, MOBA_BLOCK, MOBA_QW), lambda i, j: (i, j, 0)),
        out_shape=jax.ShapeDtypeStruct((b, t, MOBA_QW), F32),
        scratch_shapes=[pltpu.VMEM((nblk, MOBA_KVW), F32),
                        pltpu.VMEM((nblk, MOBA_KV_HEADS, rows, MOBA_BLOCK), F32),
                        group(MOBA_BLOCK), group(LANES), group(LANES), group(MOBA_DH)],
        compiler_params=_params("parallel", "arbitrary"),
        name="moba_prompt",
    )(q, k, v, z)


def _row_pages(pool_hbm, layer, buf):
    rows = pool_hbm.shape[2]

    def make(page, slot, j, sem):
        return pltpu.make_async_copy(pool_hbm.at[layer, page],
                                     buf.at[slot, pl.ds(pl.multiple_of(j * rows, rows), rows), :], sem)
    return make


def _lane_pages(pool_hbm, layer, buf):
    keys = pool_hbm.shape[3]

    def make(page, slot, j, sem):
        return pltpu.make_async_copy(pool_hbm.at[layer, page],
                                     buf.at[slot, :, pl.ds(pl.multiple_of(j * keys, keys), keys)], sem)
    return make


def _paged_pipeline(pt_ref, copies, sems, n_pages):
    b = pl.program_id(0)
    nb = pl.num_programs(0)
    slot = b % 2

    def start_all(seq, sl):
        def body(j, carry):
            page = pt_ref[seq * n_pages + j]
            for make, sem in zip(copies, sems):
                make(page, sl, j, sem.at[sl]).start()
            return carry
        lax.fori_loop(0, n_pages, body, 0, unroll=math.gcd(n_pages, DMA_UNROLL))

    @pl.when(b == 0)
    def _():
        start_all(0, 0)

    @pl.when(b + 1 < nb)
    def _():
        start_all(b + 1, 1 - slot)

    def wait_body(j, carry):
        for make, sem in zip(copies, sems):
            make(0, slot, j, sem.at[slot]).wait()
        return carry
    lax.fori_loop(0, n_pages, wait_body, 0, unroll=math.gcd(n_pages, DMA_UNROLL))
    return slot


def _moba_sample_kernel(pt_ref, q_ref, kn_ref, vn_ref, z_ref, kc_hbm, vc_hbm, o_ref,
                        kbuf, vbuf, ksem, vsem, *, layer, n_pages, ds):
    slot = _paged_pipeline(pt_ref, (_row_pages(kc_hbm, layer, kbuf), _row_pages(vc_hbm, layer, vbuf)),
                           (ksem, vsem), n_pages)
    length = n_pages * PAGE_SIZE
    nblk = length // MOBA_BLOCK
    rows = q_ref.shape[2]
    expand = _block_expand(nblk, length)
    tq = lax.broadcasted_iota(jnp.int32, (rows, SUBLANES), 0) & (SUBLANES - 1)
    cn = lax.broadcasted_iota(jnp.int32, (rows, SUBLANES), 1)
    own_ok = jnp.logical_and(cn <= jnp.minimum(tq, ds - 1), cn < ds)
    scale = MOBA_DH ** -0.5
    for g in range(MOBA_KV_HEADS):
        lanes = slice(g * MOBA_DH, (g + 1) * MOBA_DH)
        head_rows = pl.ds(g, length, stride=MOBA_KV_HEADS)
        kg = kbuf[slot, head_rows, :]
        kmean = jnp.sum(kg.reshape(nblk, MOBA_BLOCK, MOBA_DH), axis=1) * (1.0 / MOBA_BLOCK)
        qg = q_ref[0, g]
        sel = _moba_select(_dot_nt(kmean, qg, HI), nblk)
        sel_cols = _dot_tn(sel, expand)
        q_b = qg.astype(BF16)
        s = jnp.where(sel_cols > 0.5, _dot_nt(q_b, kg.astype(BF16)) * scale, NEG_INF)
        s_own = _dot_nt(q_b, kn_ref[0, :, lanes].astype(BF16)) * scale
        s_own = jnp.where(own_ok, s_own, NEG_INF)
        m = jnp.maximum(jnp.max(s, axis=-1, keepdims=True), jnp.max(s_own, axis=-1, keepdims=True))
        p = jnp.exp(s - m)
        p_own = jnp.exp(s_own - m)
        l = jnp.sum(p, axis=-1, keepdims=True) + jnp.sum(p_own, axis=-1, keepdims=True)
        o = (_dot(p.astype(BF16), vbuf[slot, head_rows, :].astype(BF16))
             + _dot(p_own.astype(BF16), vn_ref[0, :, lanes].astype(BF16)))
        o_ref[0, g] = o / l * _silu(z_ref[0, g])


def _group_rows(x, ds):
    db = x.shape[0]
    per = MOBA_HEADS // MOBA_KV_HEADS
    x = x.reshape(db, ds, MOBA_KV_HEADS, per, MOBA_DH).transpose(0, 2, 3, 1, 4)
    x = jnp.pad(x, ((0, 0), (0, 0), (0, 0), (0, SUBLANES - ds), (0, 0)))
    return x.reshape(db, MOBA_KV_HEADS, per * SUBLANES, MOBA_DH)


def _ungroup_rows(x, ds):
    db = x.shape[0]
    per = MOBA_HEADS // MOBA_KV_HEADS
    x = x.reshape(db, MOBA_KV_HEADS, per, SUBLANES, MOBA_DH)[:, :, :, :ds]
    return x.transpose(0, 3, 1, 2, 4).reshape(db, ds, MOBA_QW)


def _moba_sample(q, k_new, v_new, z, cache_k, cache_v, page_table, layer):
    db, ds, _ = q.shape
    n_pages = page_table.shape[1]
    length = n_pages * PAGE_SIZE
    buf_rows = length * MOBA_KV_HEADS
    assert length % MOBA_BLOCK == 0 and ds <= SUBLANES and length // MOBA_BLOCK >= MOBA_TOPK
    rows = (MOBA_HEADS // MOBA_KV_HEADS) * SUBLANES
    pad_new = lambda a: jnp.pad(a, ((0, 0), (0, SUBLANES - ds), (0, 0)))
    row_spec = pl.BlockSpec((1, MOBA_KV_HEADS, rows, MOBA_DH), lambda b, pt: (b, 0, 0, 0))
    new_spec = pl.BlockSpec((1, SUBLANES, MOBA_KVW), lambda b, pt: (b, 0, 0))
    out = pl.pallas_call(
        functools.partial(_moba_sample_kernel, layer=layer, n_pages=n_pages, ds=ds),
        grid_spec=pltpu.PrefetchScalarGridSpec(
            num_scalar_prefetch=1,
            grid=(db,),
            in_specs=[row_spec, new_spec, new_spec, row_spec,
                      pl.BlockSpec(memory_space=pl.ANY), pl.BlockSpec(memory_space=pl.ANY)],
            out_specs=row_spec,
            scratch_shapes=[pltpu.VMEM((2, buf_rows, MOBA_DH), F32),
                            pltpu.VMEM((2, buf_rows, MOBA_DH), F32),
                            pltpu.SemaphoreType.DMA((2,)),
                            pltpu.SemaphoreType.DMA((2,))]),
        out_shape=jax.ShapeDtypeStruct((db, MOBA_KV_HEADS, rows, MOBA_DH), F32),
        compiler_params=_params("arbitrary"),
        name="moba_sample",
    )(page_table.reshape(-1), _group_rows(q, ds), pad_new(k_new), pad_new(v_new), _group_rows(z, ds),
      cache_k, cache_v)
    return _ungroup_rows(out, ds)


def _mla_prep_kernel(cq_ref, ckv_ref, kra_ref, krb_ref, cos_ref, sin_ref, qn_ref, kvn_ref, wuq_ref, wukv_ref,
                     qcat_ref, ckv_out_ref, kr_out_ref, kcat_ref):
    cos = cos_ref[...]
    sin = sin_ref[...]
    ckv = _rms(ckv_ref[...], kvn_ref[...])
    ckv_out_ref[...] = ckv
    kr = kra_ref[...] * cos + krb_ref[...] * sin
    kr_out_ref[...] = kr[:, :MLA_ROPE]
    kcat_ref[:, :MLA_KV_RANK] = ckv.astype(BF16)
    kcat_ref[:, MLA_KV_RANK:] = kr.astype(BF16)
    xq = _dot(_rms(cq_ref[...], qn_ref[...]).astype(BF16), wuq_ref[...])
    nope_w = MLA_HEADS * MLA_NOPE
    rope_w = MLA_HEADS * LANES
    for h in range(MLA_HEADS):
        q_nope = xq[:, h * MLA_NOPE:(h + 1) * MLA_NOPE].astype(BF16)
        w_uk = wukv_ref[:, h * (MLA_NOPE + MLA_V):h * (MLA_NOPE + MLA_V) + MLA_NOPE]
        qcat_ref[h, :, :MLA_KV_RANK] = _dot_nt(q_nope, w_uk).astype(BF16)
        ra = xq[:, nope_w + h * LANES:nope_w + (h + 1) * LANES]
        rb = xq[:, nope_w + rope_w + h * LANES:nope_w + rope_w + (h + 1) * LANES]
        qcat_ref[h, :, MLA_KV_RANK:] = (ra * cos + rb * sin).astype(BF16)


def _mla_prep(cq, ckv_raw, kr_a, kr_b, cos, sin, q_norm, kv_norm, w_uq_aug, w_ukv, tm=TOKEN_TILE):
    n = cq.shape[0]
    tm = min(tm, n)
    p = cos.shape[0]
    assert n % tm == 0 and p % tm == 0
    tok = lambda w: pl.BlockSpec((tm, w), lambda i: (i, 0))
    tab = pl.BlockSpec((tm, LANES), lambda i: (i % (p // tm), 0))
    full = lambda a: pl.BlockSpec(a.shape, lambda i: (0,) * a.ndim)
    qn = q_norm.reshape(1, -1)
    kvn = kv_norm.reshape(1, -1)
    return pl.pallas_call(
        _mla_prep_kernel,
        grid=(n // tm,),
        in_specs=[tok(MLA_Q_RANK), tok(MLA_KV_RANK), tok(LANES), tok(LANES), tab, tab,
                  full(qn), full(kvn), full(w_uq_aug), full(w_ukv)],
        out_specs=[pl.BlockSpec((MLA_HEADS, tm, MLA_KCAT), lambda i: (0, i, 0)),
                   tok(MLA_KV_RANK), tok(MLA_ROPE), tok(MLA_KCAT)],
        out_shape=[jax.ShapeDtypeStruct((MLA_HEADS, n, MLA_KCAT), BF16),
                   jax.ShapeDtypeStruct((n, MLA_KV_RANK), F32),
                   jax.ShapeDtypeStruct((n, MLA_ROPE), F32),
                   jax.ShapeDtypeStruct((n, MLA_KCAT), BF16)],
        compiler_params=_params("parallel"),
        name="mla_prep",
    )(cq, ckv_raw, kr_a, kr_b, cos, sin, qn, kvn, w_uq_aug, w_ukv)


def _mla_prompt_kernel(q_ref, k_ref, o_ref, s_sc, mx_sc, mc_sc, l_sc, acc_sc, *, tq):
    i = pl.program_id(1)
    rows = MLA_HEADS * tq
    c = (MLA_NOPE + MLA_ROPE) ** -0.5 * math.log2(math.e)
    q = q_ref[:, 0].reshape(rows, MLA_KCAT)
    pos = lax.broadcasted_iota(jnp.int32, (rows, tq), 0) & (tq - 1)
    key = lax.broadcasted_iota(jnp.int32, (rows, tq), 1)

    def keys(j):
        return k_ref[0, pl.ds(pl.multiple_of(j * tq, tq), tq), :]

    s_own = jnp.where(key <= pos, _dot_nt(q, keys(i)), NEG_INF)
    s_sc[i] = s_own
    mx_sc[...] = s_own

    def sweep_scores(j, carry):
        s = _dot_nt(q, keys(j))
        s_sc[j] = s
        mx_sc[...] = jnp.maximum(mx_sc[...], s)
        return carry
    lax.fori_loop(0, i, sweep_scores, 0)
    mc_sc[...] = jnp.broadcast_to(jnp.max(mx_sc[...], axis=-1, keepdims=True) * c, (rows, LANES))

    def accumulate(j, first):
        s = s_sc[j]
        mc = mc_sc[...]
        p = [jnp.exp2(s[:, w * LANES:(w + 1) * LANES] * c - mc) for w in range(tq // LANES)]
        l_new = functools.reduce(lambda x, y: x + y, p)
        pv = _dot(jnp.concatenate(p, axis=-1).astype(BF16), keys(j)[:, :MLA_KV_RANK])
        l_sc[...] = l_new if first else l_sc[...] + l_new
        acc_sc[...] = pv if first else acc_sc[...] + pv

    accumulate(i, True)

    def sweep_acc(j, carry):
        accumulate(j, False)
        return carry
    lax.fori_loop(0, i, sweep_acc, 0)
    l = jnp.sum(l_sc[...], axis=-1, keepdims=True)
    o_ref[:, 0] = (acc_sc[...] / l).astype(o_ref.dtype).reshape(MLA_HEADS, tq, MLA_KV_RANK)


def _mla_prompt(q_cat, k_cat, tq=256):
    _, b, t, _ = q_cat.shape
    tq = min(tq, t)
    assert t % tq == 0 and tq & (tq - 1) == 0 and tq % LANES == 0
    rows = MLA_HEADS * tq
    return pl.pallas_call(
        functools.partial(_mla_prompt_kernel, tq=tq),
        grid=(b, t // tq),
        in_specs=[pl.BlockSpec((MLA_HEADS, 1, tq, MLA_KCAT), lambda i, j: (0, i, j, 0)),
                  pl.BlockSpec((1, t, MLA_KCAT), lambda i, j: (i, 0, 0))],
        out_specs=pl.BlockSpec((MLA_HEADS, 1, tq, MLA_KV_RANK), lambda i, j: (0, i, j, 0)),
        out_shape=jax.ShapeDtypeStruct((MLA_HEADS, b, t, MLA_KV_RANK), BF16),
        scratch_shapes=[pltpu.VMEM((t // tq, rows, tq), F32),
                        pltpu.VMEM((rows, tq), F32), pltpu.VMEM((rows, LANES), F32),
                        pltpu.VMEM((rows, LANES), F32), pltpu.VMEM((rows, MLA_KV_RANK), F32)],
        compiler_params=_params("parallel", "parallel"),
        name="mla_prompt",
    )(q_cat, k_cat)


def _mla_sample_kernel(pt_ref, q_ref, kn_ref, cc_hbm, rc_hbm, o_ref, cbuf, rbuf, csem, rsem,
                       *, layer, n_pages, ds):
    slot = _paged_pipeline(pt_ref, (_row_pages(cc_hbm, layer, cbuf), _lane_pages(rc_hbm, layer, rbuf)),
                           (csem, rsem), n_pages)
    rows = MLA_HEADS * SUBLANES
    scale = (MLA_NOPE + MLA_ROPE) ** -0.5
    q = q_ref[:, 0].reshape(rows, MLA_KCAT)
    q_lat = q[:, :MLA_KV_RANK]
    q_rope = q[:, MLA_KV_RANK:MLA_KV_RANK + MLA_ROPE]
    kn = kn_ref[0]
    s_own = _dot_nt(q, kn) * scale
    tq = lax.broadcasted_iota(jnp.int32, (rows, SUBLANES), 0) & (SUBLANES - 1)
    cn = lax.broadcasted_iota(jnp.int32, (rows, SUBLANES), 1)
    s_own = jnp.where(jnp.logical_and(cn <= jnp.minimum(tq, ds - 1), cn < ds), s_own, NEG_INF)
    ckv = cbuf[slot].astype(BF16)
    kr_t = rbuf[slot].astype(BF16)
    s = (_dot_nt(q_lat, ckv) + _dot(q_rope, kr_t)) * scale
    m = jnp.maximum(jnp.max(s, axis=-1, keepdims=True), jnp.max(s_own, axis=-1, keepdims=True))
    p = jnp.exp(s - m)
    p_own = jnp.exp(s_own - m)
    l = jnp.sum(p, axis=-1, keepdims=True) + jnp.sum(p_own, axis=-1, keepdims=True)
    o = _dot(p.astype(BF16), ckv) + _dot(p_own.astype(BF16), kn[:, :MLA_KV_RANK])
    o_ref[:, 0] = (o / l).reshape(MLA_HEADS, SUBLANES, MLA_KV_RANK)


def _mla_sample(q_cat, k_cat_new, cache_ckv, cache_kr, page_table, layer):
    _, db, ds, _ = q_cat.shape
    n_pages = page_table.shape[1]
    length = n_pages * PAGE_SIZE
    assert ds <= SUBLANES
    q_pad = jnp.pad(q_cat, ((0, 0), (0, 0), (0, SUBLANES - ds), (0, 0)))
    k_pad = jnp.pad(k_cat_new, ((0, 0), (0, SUBLANES - ds), (0, 0)))
    out = pl.pallas_call(
        functools.partial(_mla_sample_kernel, layer=layer, n_pages=n_pages, ds=ds),
        grid_spec=pltpu.PrefetchScalarGridSpec(
            num_scalar_prefetch=1,
            grid=(db,),
            in_specs=[pl.BlockSpec((MLA_HEADS, 1, SUBLANES, MLA_KCAT), lambda b, pt: (0, b, 0, 0)),
                      pl.BlockSpec((1, SUBLANES, MLA_KCAT), lambda b, pt: (b, 0, 0)),
                      pl.BlockSpec(memory_space=pl.ANY), pl.BlockSpec(memory_space=pl.ANY)],
            out_specs=pl.BlockSpec((MLA_HEADS, 1, SUBLANES, MLA_KV_RANK), lambda b, pt: (0, b, 0, 0)),
            scratch_shapes=[pltpu.VMEM((2, length, MLA_KV_RANK), F32),
                            pltpu.VMEM((2, MLA_ROPE, length), F32),
                            pltpu.SemaphoreType.DMA((2,)),
                            pltpu.SemaphoreType.DMA((2,))]),
        out_shape=jax.ShapeDtypeStruct((MLA_HEADS, db, SUBLANES, MLA_KV_RANK), F32),
        compiler_params=_params("arbitrary"),
        name="mla_sample",
    )(page_table.reshape(-1), q_pad, k_pad, cache_ckv, cache_kr)
    return out[:, :, :ds]


def _mla_out_kernel(ol_ref, z_ref, wukv_ref, wo_ref, g_ref, r_ref, o_ref):
    acc = None
    for h in range(MLA_HEADS):
        base = h * (MLA_NOPE + MLA_V) + MLA_NOPE
        o_h = _dot(ol_ref[h].astype(BF16), wukv_ref[:, base:base + MLA_V])
        o_h = o_h * _silu(z_ref[:, h * MLA_V:(h + 1) * MLA_V])
        part = _dot(o_h.astype(BF16), wo_ref[h * MLA_V:(h + 1) * MLA_V, :])
        acc = part if acc is None else acc + part
    o_ref[...] = r_ref[...] + _rms(acc, g_ref[...])


def _mla_out(o_lat, z, w_ukv, w_out, g, resid, tm=TOKEN_TILE):
    n, d = resid.shape
    tm = min(tm, n)
    return pl.pallas_call(
        _mla_out_kernel,
        grid=(n // tm,),
        in_specs=[pl.BlockSpec((MLA_HEADS, tm, MLA_KV_RANK), lambda i: (0, i, 0)),
                  pl.BlockSpec((tm, ODD_MIX), lambda i: (i, 0)),
                  pl.BlockSpec(w_ukv.shape, lambda i: (0, 0)),
                  pl.BlockSpec(w_out.shape, lambda i: (0, 0)),
                  pl.BlockSpec((1, d), lambda i: (0, 0)),
                  pl.BlockSpec((tm, d), lambda i: (i, 0))],
        out_specs=pl.BlockSpec((tm, d), lambda i: (i, 0)),
        out_shape=jax.ShapeDtypeStruct((n, d), F32),
        compiler_params=_params("parallel"),
        name="mla_out",
    )(o_lat, z, w_ukv, w_out, g.reshape(1, d), resid)


EVEN_WIDTHS = (GDN_CONV_CH, GDN_VW, MOBA_QW, MOBA_KVW, MOBA_KVW, MOBA_QW, LANES)
ODD_WIDTHS = (MLA_Q_RANK, MLA_KV_RANK, LANES, LANES, ODD_MIX)


def _even_w_in(w):
    o = 0
    parts = {}
    for name, wd in (("qkv", GDN_CONV_CH), ("za", GDN_VW), ("beta", GDN_HEADS), ("dec", GDN_HEADS),
                     ("qb", MOBA_QW), ("kb", MOBA_KVW), ("vb", MOBA_KVW), ("zb", MOBA_QW)):
        parts[name] = w[:, o:o + wd]
        o += wd
    pad = jnp.zeros((w.shape[0], LANES - 2 * GDN_HEADS), w.dtype)
    return jnp.concatenate([parts["qkv"], parts["za"], parts["qb"], parts["kb"], parts["vb"], parts["zb"],
                            parts["beta"], parts["dec"], pad], axis=1).astype(BF16)


def _rot_half_cols(w):
    half = w.shape[-1] // 2
    return jnp.concatenate([-w[..., half:], w[..., :half]], axis=-1)


def _odd_w_in(w):
    o1 = MLA_Q_RANK + MLA_KV_RANK
    w_kr = w[:, o1:o1 + MLA_ROPE]
    pad = jnp.zeros((w.shape[0], LANES - MLA_ROPE), w.dtype)
    return jnp.concatenate([w[:, :o1], w_kr, pad, _rot_half_cols(w_kr), pad, w[:, o1 + MLA_ROPE:]],
                           axis=1).astype(BF16)


def _odd_w_uq(w):
    r = w.shape[0]
    wh = w.reshape(r, MLA_HEADS, MLA_NOPE + MLA_ROPE)
    nope = wh[..., :MLA_NOPE].reshape(r, MLA_HEADS * MLA_NOPE)
    rope = wh[..., MLA_NOPE:]
    pad = jnp.zeros((r, MLA_HEADS, LANES - MLA_ROPE), w.dtype)
    ra = jnp.concatenate([rope, pad], axis=-1).reshape(r, MLA_HEADS * LANES)
    rb = jnp.concatenate([_rot_half_cols(rope), pad], axis=-1).reshape(r, MLA_HEADS * LANES)
    return jnp.concatenate([nope, ra, rb], axis=1).astype(BF16)


def _rope_tables(pos, reps):
    half = MLA_ROPE // 2
    inv = ROPE_THETA ** (-jnp.arange(half, dtype=F32) / half)
    ang = pos.astype(F32)[:, None] * inv[None, :]
    pad = jnp.zeros((pos.shape[0], LANES - MLA_ROPE), F32)
    cos = jnp.concatenate([jnp.cos(ang), jnp.cos(ang), pad], axis=1)
    sin = jnp.concatenate([jnp.sin(ang), jnp.sin(ang), pad], axis=1)
    return jnp.tile(cos, (reps, 1)), jnp.tile(sin, (reps, 1))


def _even_layer(y, pre_g, post_g, w_in, conv_w, a_log, dt_bias, gdn_norm, w_out, buf0, s0, state_layer, moba):
    b, t, d = y.shape
    n = b * t
    x2 = y.reshape(n, d)
    qkv, za, qb, kb, vb, zb, bd = _norm_matmul(x2, pre_g, w_in, EVEN_WIDTHS)
    c = min(GDN_CHUNK, -(-t // SUBLANES) * SUBLANES)
    tp = -(-t // c) * c
    seq = lambda a: jnp.pad(a.reshape(b, t, -1), ((0, 0), (0, tp - t), (0, 0)))
    o_a, s_new, buf_new = _gdn(seq(qkv), seq(za), seq(bd), conv_w, buf0, s0, state_layer,
                               a_log, dt_bias, gdn_norm, t)
    o_a = o_a[:, :t].reshape(n, GDN_VW)
    q3, k3, v3, z3 = (a.reshape(b, t, -1) for a in (qb, kb, vb, zb))
    o_b = moba(q3, k3, v3, z3).reshape(n, MOBA_QW)
    y_new = _out_proj([o_a, o_b], w_out, post_g, x2).reshape(b, t, d)
    kv_shape = (b, t, MOBA_KV_HEADS, MOBA_DH)
    return y_new, s_new, buf_new, kb.reshape(kv_shape), vb.reshape(kv_shape)


def _odd_layer(y, pre_g, post_g, w_in, q_norm, kv_norm, w_uq, w_ukv, w_out, cos, sin, attend):
    b, t, d = y.shape
    n = b * t
    x2 = y.reshape(n, d)
    cq, ckv_raw, kr_a, kr_b, z = _norm_matmul(x2, pre_g, w_in, ODD_WIDTHS)
    q_cat, ckv, krope, k_cat = _mla_prep(cq, ckv_raw, kr_a, kr_b, cos, sin, q_norm, kv_norm, w_uq, w_ukv)
    o_lat = attend(q_cat.reshape(MLA_HEADS, b, t, MLA_KCAT), k_cat.reshape(b, t, MLA_KCAT))
    y_new = _mla_out(o_lat.reshape(MLA_HEADS, n, MLA_KV_RANK), z, w_ukv, w_out, post_g, x2).reshape(b, t, d)
    return y_new, ckv.reshape(b, t, MLA_KV_RANK), krope.reshape(b, t, MLA_ROPE)


def kernel(x_prompt, x_sample, state_gdn, state_conv, cache_moba_k, cache_moba_v, cache_mla_ckv,
           cache_mla_krope, page_table, norm_pre, norm_post, even_w_in, even_conv_w, even_a_log,
           even_dt_bias, even_gdn_norm, even_w_out, odd_w_in, odd_q_norm, odd_kv_norm, odd_w_uq,
           odd_w_ukv, odd_w_out):
    b, t, _ = x_prompt.shape
    db, ds, _ = x_sample.shape
    depth = norm_pre.shape[0]
    past_len = page_table.shape[1] * PAGE_SIZE
    cos_p, sin_p = _rope_tables(jnp.arange(t, dtype=jnp.int32), 1)
    cos_s, sin_s = _rope_tables(past_len + jnp.arange(ds, dtype=jnp.int32), db)
    n_pool = cache_moba_k.shape[1]
    ck = cache_moba_k.reshape(-1, n_pool, PAGE_SIZE * MOBA_KV_HEADS, MOBA_DH)
    cv = cache_moba_v.reshape(-1, n_pool, PAGE_SIZE * MOBA_KV_HEADS, MOBA_DH)
    ckr = jnp.swapaxes(cache_mla_krope, 2, 3)

    yp, ys = x_prompt, x_sample
    outs = {k: [] for k in ("p_gdn", "p_conv", "p_k", "p_v", "p_ckv", "p_kr",
                            "s_gdn", "s_conv", "s_k", "s_v", "s_ckv", "s_kr")}
    for l in range(depth):
        i = l // 2
        if l % 2 == 0:
            w = (_even_w_in(even_w_in[i]), even_conv_w[i], even_a_log[i], even_dt_bias[i], even_gdn_norm[i],
                 even_w_out[i].astype(BF16))
            s0 = jnp.zeros((1, b, GDN_HEADS, GDN_DK, GDN_DV), state_gdn.dtype)
            buf0 = jnp.zeros((1, b, GDN_CONV - 1, GDN_CONV_CH), x_prompt.dtype)
            yp, st, bf, kk, vv = _even_layer(yp, norm_pre[l], norm_post[l], *w, buf0, s0, 0, _moba_prompt)
            outs["p_gdn"].append(st); outs["p_conv"].append(bf); outs["p_k"].append(kk); outs["p_v"].append(vv)
            moba_s = lambda q, k, v, z, i=i: _moba_sample(q, k, v, z, ck, cv, page_table, i)
            ys, st, bf, kk, vv = _even_layer(ys, norm_pre[l], norm_post[l], *w, state_conv, state_gdn, i, moba_s)
            outs["s_gdn"].append(st); outs["s_conv"].append(bf); outs["s_k"].append(kk); outs["s_v"].append(vv)
        else:
            w = (_odd_w_in(odd_w_in[i]), odd_q_norm[i], odd_kv_norm[i], _odd_w_uq(odd_w_uq[i]),
                 odd_w_ukv[i].astype(BF16), odd_w_out[i].astype(BF16))
            yp, c1, r1 = _odd_layer(yp, norm_pre[l], norm_post[l], *w, cos_p, sin_p, _mla_prompt)
            outs["p_ckv"].append(c1); outs["p_kr"].append(r1)
            mla_s = lambda q, k, i=i: _mla_sample(q, k, cache_mla_ckv, ckr, page_table, i)
            ys, c1, r1 = _odd_layer(ys, norm_pre[l], norm_post[l], *w, cos_s, sin_s, mla_s)
            outs["s_ckv"].append(c1); outs["s_kr"].append(r1)
    st = lambda k: jnp.stack(outs[k])
    return (yp, ys, st("p_gdn"), st("p_conv"), st("p_k"), st("p_v"), st("p_ckv"), st("p_kr"),
            st("s_gdn"), st("s_conv"), st("s_k"), st("s_v"), st("s_ckv"), st("s_kr"))
```
